```python
import jax, jax.numpy as jnp
from jax import lax
import numpy as np

D_MODEL = 1024
BATCH = 8
SEQ = 4096
DEPTH = 4

CONV_WIDTH = 3
CONV_DIM = D_MODEL
CONV_GROUPS = 8
SGU_DIM = D_MODEL
SGU_HEADS = 8
SGU_HEAD_DIM = SGU_DIM // SGU_HEADS
CHUNK = 128
N_BRANCHES = 2
D_FF = 4 * D_MODEL
IN_COLS = 3 * CONV_DIM + 2 * SGU_DIM + N_BRANCHES * D_MODEL
EPS = 1e-6

kernel_name = "hybrid_shortconv_chunked_sgu_block"

SPLIT_POINTS = [
    CONV_DIM,
    2 * CONV_DIM,
    3 * CONV_DIM,
    3 * CONV_DIM + SGU_DIM,
    3 * CONV_DIM + 2 * SGU_DIM,
    3 * CONV_DIM + 2 * SGU_DIM + D_MODEL,
]


def rms_norm(x, g):
    xf = x.astype(jnp.float32)
    y = xf * lax.rsqrt(jnp.mean(xf * xf, axis=-1, keepdims=True) + EPS)
    return (y * g.astype(jnp.float32)).astype(x.dtype)


def layer_norm(x, g, b):
    xf = x.astype(jnp.float32)
    mu = jnp.mean(xf, axis=-1, keepdims=True)
    xc = xf - mu
    var = jnp.mean(xc * xc, axis=-1, keepdims=True)
    y = xc * lax.rsqrt(var + EPS) * g.astype(jnp.float32) + b.astype(jnp.float32)
    return y.astype(x.dtype)


def causal_depthwise_conv(z, w):
    seq = z.shape[1]
    zp = jnp.pad(z, ((0, 0), (CONV_WIDTH - 1, 0), (0, 0)))
    y = w[0] * zp[:, 0:seq]
    for k in range(1, CONV_WIDTH):
        y = y + w[k] * zp[:, k:k + seq]
    return y


def chunked_spatial_gating(u, v, w_s, b_s, ln_g, ln_b):
    bsz, seq, _ = v.shape
    n_chunks = seq // CHUNK
    vn = layer_norm(v, ln_g, ln_b).reshape(bsz, n_chunks, CHUNK, SGU_HEADS, SGU_HEAD_DIM)
    causal_mask = jnp.tril(jnp.ones((CHUNK, CHUNK), dtype=w_s.dtype))
    mixed = jnp.einsum('hts,bnshd->bnthd', w_s * causal_mask, vn)
    mixed = mixed + jnp.transpose(b_s)[None, None, :, :, None]
    return u * mixed.reshape(bsz, seq, SGU_DIM)


def setup_inputs(seed: int = 0) -> dict:
    key = jax.random.key(seed)
    ks = jax.random.split(key, 14)
    f32 = jnp.float32
    x = jax.random.normal(ks[0], (BATCH, SEQ, D_MODEL), f32)
    norm_mix = 1.0 + 0.02 * jax.random.normal(ks[1], (DEPTH, D_MODEL), f32)
    w_in = jax.random.normal(ks[2], (DEPTH, D_MODEL, IN_COLS), f32) * D_MODEL ** -0.5
    conv_w = jax.random.normal(ks[3], (DEPTH, CONV_WIDTH, CONV_DIM), f32) * CONV_WIDTH ** -0.5
    sgu_w = jax.random.normal(ks[4], (DEPTH, SGU_HEADS, CHUNK, CHUNK), f32) * CHUNK ** -0.5
    sgu_b = 1.0 + 0.02 * jax.random.normal(ks[5], (DEPTH, SGU_HEADS, CHUNK), f32)
    sgu_ln_g = 1.0 + 0.02 * jax.random.normal(ks[6], (DEPTH, SGU_DIM), f32)
    sgu_ln_b = 0.02 * jax.random.normal(ks[7], (DEPTH, SGU_DIM), f32)
    w_out = jax.random.normal(ks[8], (DEPTH, D_MODEL, D_MODEL), f32) * D_MODEL ** -0.5
    norm_mlp = 1.0 + 0.02 * jax.random.normal(ks[9], (DEPTH, D_MODEL), f32)
    w_ff1 = jax.random.normal(ks[10], (DEPTH, D_MODEL, D_FF), f32) * D_MODEL ** -0.5
    w_ff2 = jax.random.normal(ks[11], (DEPTH, D_FF, D_MODEL), f32) * D_FF ** -0.5
    final_norm = 1.0 + 0.02 * jax.random.normal(ks[12], (D_MODEL,), f32)
    return {
        "x": x, "norm_mix": norm_mix, "w_in": w_in, "conv_w": conv_w,
        "sgu_w": sgu_w, "sgu_b": sgu_b, "sgu_ln_g": sgu_ln_g, "sgu_ln_b": sgu_ln_b,
        "w_out": w_out, "norm_mlp": norm_mlp, "w_ff1": w_ff1, "w_ff2": w_ff2,
        "final_norm": final_norm,
    }


def reference(x, norm_mix, w_in, conv_w, sgu_w, sgu_b, sgu_ln_g, sgu_ln_b,
              w_out, norm_mlp, w_ff1, w_ff2, final_norm):
    for l in range(DEPTH):
        h = rms_norm(x, norm_mix[l])
        proj = jnp.einsum('bsd,dc->bsc', h, w_in[l])
        c_gate, b_gate, a_in, u, v, g_a, g_b = jnp.split(proj, SPLIT_POINTS, axis=-1)
        y_a = b_gate * causal_depthwise_conv(c_gate * a_in, conv_w[l])
        y_b = chunked_spatial_gating(jax.nn.gelu(u), jax.nn.gelu(v), sgu_w[l], sgu_b[l],
                                     sgu_ln_g[l], sgu_ln_b[l])
        merged = jax.nn.sigmoid(g_a) * y_a + jax.nn.sigmoid(g_b) * y_b
        x = x + jnp.einsum('bsd,de->bse', merged, w_out[l])
        h = rms_norm(x, norm_mlp[l])
        hid = jnp.square(jax.nn.relu(jnp.einsum('bsd,df->bsf', h, w_ff1[l])))
        x = x + jnp.einsum('bsf,fd->bsd', hid, w_ff2[l])
    return rms_norm(x, final_norm)
```

```python
import functools

import jax
import jax.numpy as jnp
from jax import lax
from jax.experimental import pallas as pl
from jax.experimental.pallas import tpu as pltpu

D_MODEL = 1024
CONV_WIDTH = 3
SGU_HEADS = 8
HEAD_DIM = D_MODEL // SGU_HEADS
CHUNK = 128
D_FF = 4 * D_MODEL
N_GROUPS = 7
EPS = 1e-6

SUBLANES = 8
MIX_ROWS = 512
FFN_ROWS = 512
FF_COLS = 1024
VMEM_LIMIT_BYTES = 56 * 1024 * 1024


def _rms_norm(x, g):
    return x * lax.rsqrt(jnp.mean(x * x, axis=-1, keepdims=True) + EPS) * g


def _dot(a, b):
    return jnp.dot(a, b, preferred_element_type=jnp.float32)


def _mixer_kernel(x_ref, g_ref, w_in_ref, conv_w_ref, sgu_w_ref, sgu_b_ref, ln_g_ref, ln_b_ref,
                  w_out_ref, o_ref, z_ref):
    rows = x_ref.shape[1]
    n_chunks = rows // CHUNK

    x = x_ref[0]
    h = _rms_norm(x, g_ref[...]).astype(jnp.bfloat16)

    def proj(k):
        return _dot(h, w_in_ref[:, k * D_MODEL:(k + 1) * D_MODEL])

    @pl.when(pl.program_id(1) == 0)
    def _():
        z_ref[0:SUBLANES, :] = jnp.zeros((SUBLANES, D_MODEL), jnp.float32)

    z_ref[SUBLANES:SUBLANES + rows, :] = proj(0) * proj(2)
    cw = conv_w_ref[...]
    conv = cw[2:3] * z_ref[SUBLANES:SUBLANES + rows, :]
    conv += cw[1:2] * z_ref[SUBLANES - 1:SUBLANES - 1 + rows, :]
    conv += cw[0:1] * z_ref[SUBLANES - 2:SUBLANES - 2 + rows, :]
    z_ref[0:SUBLANES, :] = z_ref[rows:rows + SUBLANES, :]
    merged = jax.nn.sigmoid(proj(5)) * (proj(1) * conv)

    gv = jax.nn.gelu(proj(4))
    mu = jnp.mean(gv, axis=-1, keepdims=True)
    gc = gv - mu
    var = jnp.mean(gc * gc, axis=-1, keepdims=True)
    vn = (gc * lax.rsqrt(var + EPS) * ln_g_ref[...] + ln_b_ref[...]).astype(jnp.bfloat16)

    t_idx = lax.broadcasted_iota(jnp.int32, (CHUNK, CHUNK), 0)
    s_idx = lax.broadcasted_iota(jnp.int32, (CHUNK, CHUNK), 1)
    causal = s_idx <= t_idx
    bias = sgu_b_ref[...]
    mixed_heads = []
    for hd in range(SGU_HEADS):
        w = jnp.where(causal, sgu_w_ref[hd], 0.0).astype(jnp.bfloat16)
        cols = slice(hd * HEAD_DIM, (hd + 1) * HEAD_DIM)
        v_h = jnp.concatenate([vn[c * CHUNK:(c + 1) * CHUNK, cols] for c in range(n_chunks)], axis=1)
        m_h = _dot(w, v_h) + bias[:, hd:hd + 1]
        mixed_heads.append(
            jnp.concatenate([m_h[:, c * HEAD_DIM:(c + 1) * HEAD_DIM] for c in range(n_chunks)], axis=0))
    mixed = jnp.concatenate(mixed_heads, axis=1)

    merged += jax.nn.sigmoid(proj(6)) * (jax.nn.gelu(proj(3)) * mixed)
    o_ref[0] = x + _dot(merged.astype(jnp.bfloat16), w_out_ref[...])


def _ffn_kernel(x_ref, g_ref, w1_ref, w2_ref, gf_ref, o_ref, *, final):
    x = x_ref[...]
    h = _rms_norm(x, g_ref[...]).astype(jnp.bfloat16)
    acc = x
    for j in range(D_FF // FF_COLS):
        cols = slice(j * FF_COLS, (j + 1) * FF_COLS)
        hid = jnp.square(jnp.maximum(_dot(h, w1_ref[:, cols]), 0.0)).astype(jnp.bfloat16)
        acc += _dot(hid, w2_ref[cols, :])
    if final:
        acc = _rms_norm(acc, gf_ref[...])
    o_ref[...] = acc


def _resident(shape):
    return pl.BlockSpec(shape, lambda *_: (0,) * len(shape), pipeline_mode=pl.Buffered(1))


def _mixer(x, g, w_in, conv_w, sgu_w, sgu_b_t, ln_g, ln_b, w_out):
    batch, seq, d = x.shape
    rows = MIX_ROWS
    xspec = pl.BlockSpec((1, rows, d), lambda b, s: (b, s, 0))
    return pl.pallas_call(
        _mixer_kernel,
        grid=(batch, seq // rows),
        in_specs=[xspec, _resident(g.shape), _resident(w_in.shape), _resident(conv_w.shape),
                  _resident(sgu_w.shape), _resident(sgu_b_t.shape), _resident(ln_g.shape),
                  _resident(ln_b.shape), _resident(w_out.shape)],
        out_specs=xspec,
        out_shape=jax.ShapeDtypeStruct(x.shape, x.dtype),
        scratch_shapes=[pltpu.VMEM((rows + SUBLANES, d), jnp.float32)],
        compiler_params=pltpu.CompilerParams(
            dimension_semantics=("arbitrary", "arbitrary"), vmem_limit_bytes=VMEM_LIMIT_BYTES),
        name="mixer",
    )(x, g, w_in, conv_w, sgu_w, sgu_b_t, ln_g, ln_b, w_out)


def _ffn(x2, g, w1, w2, gf, final):
    tokens, d = x2.shape
    rows = FFN_ROWS
    xspec = pl.BlockSpec((rows, d), lambda i: (i, 0))
    return pl.pallas_call(
        functools.partial(_ffn_kernel, final=final),
        grid=(tokens // rows,),
        in_specs=[xspec, _resident(g.shape), _resident(w1.shape), _resident(w2.shape), _resident(gf.shape)],
        out_specs=xspec,
        out_shape=jax.ShapeDtypeStruct(x2.shape, x2.dtype),
        compiler_params=pltpu.CompilerParams(
            dimension_semantics=("arbitrary",), vmem_limit_bytes=VMEM_LIMIT_BYTES),
        name="ffn_final" if final else "ffn",
    )(x2, g, w1, w2, gf)


def kernel(x, norm_mix, w_in, conv_w, sgu_w, sgu_b, sgu_ln_g, sgu_ln_b, w_out, norm_mlp, w_ff1, w_ff2, final_norm):
    batch, seq, d = x.shape
    depth = w_in.shape[0]
    bf16 = jnp.bfloat16
    w_in_b, w_out_b, w1_b, w2_b = (w.astype(bf16) for w in (w_in, w_out, w_ff1, w_ff2))
    gf = final_norm.reshape(1, d)
    for l in range(depth):
        x = _mixer(x, norm_mix[l].reshape(1, d), w_in_b[l], conv_w[l], sgu_w[l], sgu_b[l].T,
                   sgu_ln_g[l].reshape(1, d), sgu_ln_b[l].reshape(1, d), w_out_b[l])
        x = _ffn(x.reshape(batch * seq, d), norm_mlp[l].reshape(1, d), w1_b[l], w2_b[l], gf,
                 final=(l == depth - 1)).reshape(batch, seq, d)
    return x
```

```python
import functools

import jax
import jax.numpy as jnp
from jax import lax
from jax.experimental import pallas as pl
from jax.experimental.pallas import tpu as pltpu

D_MODEL = 1024
SGU_HEADS = 8
HEAD_DIM = D_MODEL // SGU_HEADS
CHUNK = 128
D_FF = 4 * D_MODEL
EPS = 1e-6

SUBLANES = 8
BF16_ROWS = 16
MIX_ROWS = 512
FFN_ROWS = 512
FF_COLS = 1024
VMEM_LIMIT_BYTES = 56 * 1024 * 1024


def _rms_norm(x, g):
    return x * lax.rsqrt(jnp.mean(x * x, axis=-1, keepdims=True) + EPS) * g


def _dot(a, b):
    return jnp.dot(a, b, preferred_element_type=jnp.float32)


def _cast_chunks(src_refs, dst_refs):
    for src, dst in zip(src_refs, dst_refs):
        dst[...] = src[...].astype(dst.dtype)


def _mixer_kernel(x_ref, g_ref, w_in_ref, conv_w_ref, sgu_w_ref, sgu_b_ref, ln_g_ref, ln_b_ref,
                  w_out_ref, w1_src, w2_src, o_ref, w1_dst, w2_dst, z_ref, *, blocks_per_seq):
    rows = x_ref.shape[0]
    n_chunks = rows // CHUNK
    half = rows // 2

    @pl.when(pl.program_id(0) % blocks_per_seq == 0)
    def _():
        z_ref[0:SUBLANES, :] = jnp.zeros((SUBLANES, D_MODEL), jnp.float32)

    _cast_chunks((w1_src, w2_src), (w1_dst, w2_dst))

    x = x_ref[...]
    h = _rms_norm(x, g_ref[...]).astype(jnp.bfloat16)

    def proj(k, hk=h):
        return _dot(hk, w_in_ref[:, k * D_MODEL:(k + 1) * D_MODEL])

    gv = jax.nn.gelu(proj(4))
    mu = jnp.mean(gv, axis=-1, keepdims=True)
    gc = gv - mu
    var = jnp.mean(gc * gc, axis=-1, keepdims=True)
    vn = (gc * lax.rsqrt(var + EPS) * ln_g_ref[...] + ln_b_ref[...]).astype(jnp.bfloat16)

    z_ref[SUBLANES:SUBLANES + rows, :] = proj(0) * proj(2)

    t_idx = lax.broadcasted_iota(jnp.int32, (CHUNK, CHUNK), 0)
    s_idx = lax.broadcasted_iota(jnp.int32, (CHUNK, CHUNK), 1)
    causal = s_idx <= t_idx
    bias = sgu_b_ref[...]
    mixed_heads = []
    for hd in range(SGU_HEADS):
        w = jnp.where(causal, sgu_w_ref[hd], 0.0).astype(jnp.bfloat16)
        cols = slice(hd * HEAD_DIM, (hd + 1) * HEAD_DIM)
        v_h = jnp.concatenate([vn[c * CHUNK:(c + 1) * CHUNK, cols] for c in range(n_chunks)], axis=1)
        m_h = _dot(w, v_h) + bias[:, hd:hd + 1]
        mixed_heads.append(
            jnp.concatenate([m_h[:, c * HEAD_DIM:(c + 1) * HEAD_DIM] for c in range(n_chunks)], axis=0))
    mixed = jnp.concatenate(mixed_heads, axis=1)

    p1 = proj(1)
    cw = conv_w_ref[...]
    conv = cw[2:3] * z_ref[SUBLANES:SUBLANES + rows, :]
    conv += cw[1:2] * z_ref[SUBLANES - 1:SUBLANES - 1 + rows, :]
    conv += cw[0:1] * z_ref[SUBLANES - 2:SUBLANES - 2 + rows, :]
    z_ref[0:SUBLANES, :] = z_ref[rows:rows + SUBLANES, :]
    y_a = jax.nn.sigmoid(proj(5)) * (p1 * conv)
    y_b = jax.nn.gelu(proj(3)) * mixed

    merged = []
    for r in range(2):
        rs = slice(r * half, (r + 1) * half)
        merged.append((y_a[rs] + jax.nn.sigmoid(proj(6, h[rs])) * y_b[rs]).astype(jnp.bfloat16))
    for r in range(2):
        rs = slice(r * half, (r + 1) * half)
        o_ref[rs, :] = x[rs] + _dot(merged[r], w_out_ref[...])


def _ffn_kernel(x_ref, g_ref, w1_ref, w2_ref, gf_ref, *rest, final):
    n_cast = (len(rest) - 1) // 2
    o_ref = rest[n_cast]
    _cast_chunks(rest[:n_cast], rest[n_cast + 1:])

    x = x_ref[...]
    h = _rms_norm(x, g_ref[...]).astype(jnp.bfloat16)
    acc = x
    for j in range(D_FF // FF_COLS):
        cols = slice(j * FF_COLS, (j + 1) * FF_COLS)
        hid = jnp.square(jnp.maximum(_dot(h, w1_ref[:, cols]), 0.0)).astype(jnp.bfloat16)
        acc += _dot(hid, w2_ref[cols, :])
    if final:
        acc = _rms_norm(acc, gf_ref[...])
    o_ref[...] = acc


def _resident(arr):
    zeros = (0,) * arr.ndim
    return pl.BlockSpec(arr.shape, lambda i: zeros, pipeline_mode=pl.Buffered(1))


def _layer_resident(arr, layer):
    zeros = (0,) * (arr.ndim - 1)
    return pl.BlockSpec((None,) + arr.shape[1:], lambda i: (layer,) + zeros, pipeline_mode=pl.Buffered(1))


def _cast_specs(stack, layer, steps):
    _, k, n = stack.shape
    chunk = k // steps
    assert chunk * steps == k and chunk % BF16_ROWS == 0, (stack.shape, steps)
    src = pl.BlockSpec((None, chunk, n), lambda i: (layer, i, 0))
    dst = pl.BlockSpec((chunk, n), lambda i: (i, 0))
    return src, dst, jax.ShapeDtypeStruct((k, n), jnp.bfloat16)


def _mixer(x2, layer, seq, g, w_in_b, conv_w, sgu_w, sgu_b_t, ln_g, ln_b, w_out_b, w_ff1, w_ff2):
    tokens, d = x2.shape
    rows = MIX_ROWS
    steps = tokens // rows
    xspec = pl.BlockSpec((rows, d), lambda i: (i, 0))
    stacked = lambda p: _layer_resident(p, layer)
    casts = [_cast_specs(w, layer, steps) for w in (w_ff1, w_ff2)]
    return pl.pallas_call(
        functools.partial(_mixer_kernel, blocks_per_seq=seq // rows),
        grid=(steps,),
        in_specs=[xspec, stacked(g), _resident(w_in_b), stacked(conv_w), stacked(sgu_w), stacked(sgu_b_t),
                  stacked(ln_g), stacked(ln_b), _resident(w_out_b)] + [c[0] for c in casts],
        out_specs=[xspec] + [c[1] for c in casts],
        out_shape=[jax.ShapeDtypeStruct(x2.shape, x2.dtype)] + [c[2] for c in casts],
        scratch_shapes=[pltpu.VMEM((rows + SUBLANES, d), jnp.float32)],
        compiler_params=pltpu.CompilerParams(
            dimension_semantics=("arbitrary",), vmem_limit_bytes=VMEM_LIMIT_BYTES),
        name="mixer",
    )(x2, g, w_in_b, conv_w, sgu_w, sgu_b_t, ln_g, ln_b, w_out_b, w_ff1, w_ff2)


def _ffn(x2, layer, g, w1_b, w2_b, gf, next_weights):
    tokens, d = x2.shape
    rows = FFN_ROWS
    steps = tokens // rows
    xspec = pl.BlockSpec((rows, d), lambda i: (i, 0))
    casts = [_cast_specs(w, layer + 1, steps) for w in next_weights]
    final = not next_weights
    return pl.pallas_call(
        functools.partial(_ffn_kernel, final=final),
        grid=(steps,),
        in_specs=[xspec, _layer_resident(g, layer), _resident(w1_b), _resident(w2_b), _resident(gf)]
                 + [c[0] for c in casts],
        out_specs=[xspec] + [c[1] for c in casts],
        out_shape=[jax.ShapeDtypeStruct(x2.shape, x2.dtype)] + [c[2] for c in casts],
        compiler_params=pltpu.CompilerParams(
            dimension_semantics=("arbitrary",), vmem_limit_bytes=VMEM_LIMIT_BYTES),
        name="ffn_final" if final else "ffn",
    )(x2, g, w1_b, w2_b, gf, *next_weights)


def kernel(x, norm_mix, w_in, conv_w, sgu_w, sgu_b, sgu_ln_g, sgu_ln_b, w_out, norm_mlp, w_ff1, w_ff2, final_norm):
    batch, seq, d = x.shape
    depth = w_in.shape[0]
    row = lambda p: p.reshape(depth, 1, d)
    norm_mix, norm_mlp, sgu_ln_g, sgu_ln_b = (row(p) for p in (norm_mix, norm_mlp, sgu_ln_g, sgu_ln_b))
    sgu_b_t = jnp.swapaxes(sgu_b, 1, 2)
    gf = final_norm.reshape(1, d)
    x2 = x.reshape(batch * seq, d)
    w_in_b, w_out_b = w_in[0].astype(jnp.bfloat16), w_out[0].astype(jnp.bfloat16)
    for l in range(depth):
        x2, w1_b, w2_b = _mixer(x2, l, seq, norm_mix, w_in_b, conv_w, sgu_w, sgu_b_t, sgu_ln_g, sgu_ln_b,
                                w_out_b, w_ff1, w_ff2)
        if l + 1 < depth:
            x2, w_in_b, w_out_b = _ffn(x2, l, norm_mlp, w1_b, w2_b, gf, (w_in, w_out))
        else:
            (x2,) = _ffn(x2, l, norm_mlp, w1_b, w2_b, gf, ())
    return x2.reshape(batch, seq, d)
```

```python
import functools

import jax
import jax.numpy as jnp
from jax import lax
from jax.experimental import pallas as pl
from jax.experimental.pallas import tpu as pltpu

D_MODEL = 1024
SGU_HEADS = 8
HEAD_DIM = D_MODEL // SGU_HEADS
CHUNK = 128
D_FF = 4 * D_MODEL
EPS = 1e-6

SUBLANES = 8
BF16_ROWS = 16
MIX_ROWS = 1024
FFN_ROWS = 1024
FF_COLS = 1024
GROUP_COLS = 256
VMEM_LIMIT_BYTES = 56 * 1024 * 1024


def _rms_norm(x, g):
    return x * lax.rsqrt(jnp.mean(x * x, axis=-1, keepdims=True) + EPS) * g


def _dot(a, b):
    return jnp.dot(a, b, preferred_element_type=jnp.float32)


def _cast_chunks(src_refs, dst_refs):
    for src, dst in zip(src_refs, dst_refs):
        dst[...] = src[...].astype(dst.dtype)


def _mixer_kernel(x_ref, g_ref, w_in_ref, conv_w_ref, sgu_w_ref, sgu_b_ref, ln_g_ref, ln_b_ref,
                  w_out_ref, w1_src, w2_src, o_ref, w1_dst, w2_dst, z_ref, *, blocks_per_seq):
    rows = x_ref.shape[0]
    n_chunks = rows // CHUNK
    half = rows // 2

    @pl.when(pl.program_id(0) % blocks_per_seq == 0)
    def _():
        z_ref[0:SUBLANES, :] = jnp.zeros((SUBLANES, D_MODEL), jnp.float32)

    _cast_chunks((w1_src, w2_src), (w1_dst, w2_dst))

    x = x_ref[...]
    h = _rms_norm(x, g_ref[...]).astype(jnp.bfloat16)

    def proj(k, cols, hk=h):
        return _dot(hk, w_in_ref[:, k * D_MODEL + cols.start:k * D_MODEL + cols.stop])

    cw = conv_w_ref[...]
    groups = [slice(j * GROUP_COLS, (j + 1) * GROUP_COLS) for j in range(D_MODEL // GROUP_COLS)]

    gv, gv_sum, y_a = [], 0.0, []
    for cols in groups:
        gv.append(jax.nn.gelu(proj(4, cols)))
        gv_sum += jnp.sum(gv[-1], axis=-1, keepdims=True)
        z_ref[SUBLANES:SUBLANES + rows, cols] = proj(0, cols) * proj(2, cols)
        p1 = proj(1, cols)
        conv = cw[2:3, cols] * z_ref[SUBLANES:SUBLANES + rows, cols]
        conv += cw[1:2, cols] * z_ref[SUBLANES - 1:SUBLANES - 1 + rows, cols]
        conv += cw[0:1, cols] * z_ref[SUBLANES - 2:SUBLANES - 2 + rows, cols]
        y_a.append(jax.nn.sigmoid(proj(5, cols)) * (p1 * conv))
    z_ref[0:SUBLANES, :] = z_ref[rows:rows + SUBLANES, :]

    mu = gv_sum * (1.0 / D_MODEL)
    gc = [g - mu for g in gv]
    var = sum(jnp.sum(c * c, axis=-1, keepdims=True) for c in gc) * (1.0 / D_MODEL)
    rstd = lax.rsqrt(var + EPS)

    t_idx = lax.broadcasted_iota(jnp.int32, (CHUNK, CHUNK), 0)
    s_idx = lax.broadcasted_iota(jnp.int32, (CHUNK, CHUNK), 1)
    causal = s_idx <= t_idx
    bias = sgu_b_ref[...]

    def spatial_gate(vn, hd):
        w = jnp.where(causal, sgu_w_ref[hd], 0.0).astype(jnp.bfloat16)
        v_h = jnp.concatenate([vn[c * CHUNK:(c + 1) * CHUNK] for c in range(n_chunks)], axis=1)
        m_h = _dot(w, v_h) + bias[:, hd:hd + 1]
        return jnp.concatenate([m_h[:, c * HEAD_DIM:(c + 1) * HEAD_DIM] for c in range(n_chunks)], axis=0)

    merged = []
    for j, cols in enumerate(groups):
        gu = jax.nn.gelu(proj(3, cols))
        gate_b = jax.nn.sigmoid(proj(6, cols))
        vn = (gc[j] * rstd * ln_g_ref[:, cols] + ln_b_ref[:, cols]).astype(jnp.bfloat16)
        mixed = jnp.concatenate(
            [spatial_gate(vn[:, k * HEAD_DIM:(k + 1) * HEAD_DIM], cols.start // HEAD_DIM + k)
             for k in range(GROUP_COLS // HEAD_DIM)], axis=1)
        merged.append((y_a[j] + gate_b * (gu * mixed)).astype(jnp.bfloat16))


    merged = jnp.concatenate(merged, axis=1)
    for r in range(2):
        rs = slice(r * half, (r + 1) * half)
        o_ref[rs, :] = x[rs] + _dot(merged[rs], w_out_ref[...])


def _ffn_kernel(x_ref, g_ref, w1_ref, w2_ref, gf_ref, *rest, final):
    n_cast = (len(rest) - 1) // 2
    o_ref = rest[n_cast]
    _cast_chunks(rest[:n_cast], rest[n_cast + 1:])

    x = x_ref[...]
    h = _rms_norm(x, g_ref[...]).astype(jnp.bfloat16)
    acc = x
    for j in range(D_FF // FF_COLS):
        cols = slice(j * FF_COLS, (j + 1) * FF_COLS)
        hid = jnp.square(jnp.maximum(_dot(h, w1_ref[:, cols]), 0.0)).astype(jnp.bfloat16)
        acc += _dot(hid, w2_ref[cols, :])
    if final:
        acc = _rms_norm(acc, gf_ref[...])
    o_ref[...] = acc


def _resident(arr):
    zeros = (0,) * arr.ndim
    return pl.BlockSpec(arr.shape, lambda i: zeros, pipeline_mode=pl.Buffered(1))


def _layer_resident(arr, layer):
    zeros = (0,) * (arr.ndim - 1)
    return pl.BlockSpec((None,) + arr.shape[1:], lambda i: (layer,) + zeros, pipeline_mode=pl.Buffered(1))


def _cast_specs(stack, layer, steps):
    _, k, n = stack.shape
    chunk = k // steps
    assert chunk * steps == k and chunk % BF16_ROWS == 0, (stack.shape, steps)
    src = pl.BlockSpec((None, chunk, n), lambda i: (layer, i, 0))
    dst = pl.BlockSpec((chunk, n), lambda i: (i, 0))
    return src, dst, jax.ShapeDtypeStruct((k, n), jnp.bfloat16)


def _mixer(x2, layer, seq, g, w_in_b, conv_w, sgu_w, sgu_b_t, ln_g, ln_b, w_out_b, w_ff1, w_ff2):
    tokens, d = x2.shape
    rows = MIX_ROWS
    steps = tokens // rows
    xspec = pl.BlockSpec((rows, d), lambda i: (i, 0))
    stacked = lambda p: _layer_resident(p, layer)
    casts = [_cast_specs(w, layer, steps) for w in (w_ff1, w_ff2)]
    return pl.pallas_call(
        functools.partial(_mixer_kernel, blocks_per_seq=seq // rows),
        grid=(steps,),
        in_specs=[xspec, stacked(g), _resident(w_in_b), stacked(conv_w), stacked(sgu_w), stacked(sgu_b_t),
                  stacked(ln_g), stacked(ln_b), _resident(w_out_b)] + [c[0] for c in casts],
        out_specs=[xspec] + [c[1] for c in casts],
        out_shape=[jax.ShapeDtypeStruct(x2.shape, x2.dtype)] + [c[2] for c in casts],
        scratch_shapes=[pltpu.VMEM((rows + SUBLANES, d), jnp.float32)],
        compiler_params=pltpu.CompilerParams(
            dimension_semantics=("arbitrary",), vmem_limit_bytes=VMEM_LIMIT_BYTES),
        name="mixer",
    )(x2, g, w_in_b, conv_w, sgu_w, sgu_b_t, ln_g, ln_b, w_out_b, w_ff1, w_ff2)


def _ffn(x2, layer, g, w1_b, w2_b, gf, next_weights):
    tokens, d = x2.shape
    rows = FFN_ROWS
    steps = tokens // rows
    xspec = pl.BlockSpec((rows, d), lambda i: (i, 0))
    casts = [_cast_specs(w, layer + 1, steps) for w in next_weights]
    final = not next_weights
    return pl.pallas_call(
        functools.partial(_ffn_kernel, final=final),
        grid=(steps,),
        in_specs=[xspec, _layer_resident(g, layer), _resident(w1_b), _resident(w2_b), _resident(gf)]
                 + [c[0] for c in casts],
        out_specs=[xspec] + [c[1] for c in casts],
        out_shape=[jax.ShapeDtypeStruct(x2.shape, x2.dtype)] + [c[2] for c in casts],
        compiler_params=pltpu.CompilerParams(
            dimension_semantics=("arbitrary",), vmem_limit_bytes=VMEM_LIMIT_BYTES),
        name="ffn_final" if final else "ffn",
    )(x2, g, w1_b, w2_b, gf, *next_weights)


def kernel(x, norm_mix, w_in, conv_w, sgu_w, sgu_b, sgu_ln_g, sgu_ln_b, w_out, norm_mlp, w_ff1, w_ff2, final_norm):
    batch, seq, d = x.shape
    depth = w_in.shape[0]
    row = lambda p: p.reshape(depth, 1, d)
    norm_mix, norm_mlp, sgu_ln_g, sgu_ln_b = (row(p) for p in (norm_mix, norm_mlp, sgu_ln_g, sgu_ln_b))
    sgu_b_t = jnp.swapaxes(sgu_b, 1, 2)
    gf = final_norm.reshape(1, d)
    x2 = x.reshape(batch * seq, d)
    w_in_b, w_out_b = w_in[0].astype(jnp.bfloat16), w_out[0].astype(jnp.bfloat16)
    for l in range(depth):
        x2, w1_b, w2_b = _mixer(x2, l, seq, norm_mix, w_in_b, conv_w, sgu_w, sgu_b_t, sgu_ln_g, sgu_ln_b,
                                w_out_b, w_ff1, w_ff2)
        if l + 1 < depth:
            x2, w_in_b, w_out_b = _ffn(x2, l, norm_mlp, w1_b, w2_b, gf, (w_in, w_out))
        else:
            (x2,) = _ffn(x2, l, norm_mlp, w1_b, w2_b, gf, ())
    return x2.reshape(batch, seq, d)
```

```python
import functools

import jax
import jax.numpy as jnp
from jax import lax
from jax.experimental import pallas as pl
from jax.experimental.pallas import tpu as pltpu

D_MODEL = 1024
SGU_HEADS = 8
HEAD_DIM = D_MODEL // SGU_HEADS
CHUNK = 128
D_FF = 4 * D_MODEL
EPS = 1e-6

SUBLANES = 8
BF16_ROWS = 16
MIX_ROWS = 1024
FFN_ROWS = 1024
FF_COLS = 4096
GROUP_COLS = 256
VMEM_LIMIT_BYTES = 56 * 1024 * 1024


def _rms_norm(x, g):
    return x * lax.rsqrt(jnp.mean(x * x, axis=-1, keepdims=True) + EPS) * g


def _dot(a, b):
    return jnp.dot(a, b, preferred_element_type=jnp.float32)


def _cast_chunks(src_refs, dst_refs):
    for src, dst in zip(src_refs, dst_refs):
        dst[...] = src[...].astype(dst.dtype)


def _mixer_kernel(x_ref, g_ref, w_in_ref, conv_w_ref, sgu_w_ref, sgu_b_ref, ln_g_ref, ln_b_ref,
                  w_out_ref, w1_src, w2_src, o_ref, w1_dst, w2_dst, z_ref, *, blocks_per_seq):
    rows = x_ref.shape[0]
    n_chunks = rows // CHUNK
    half = rows // 2

    @pl.when(pl.program_id(0) % blocks_per_seq == 0)
    def _():
        z_ref[0:SUBLANES, :] = jnp.zeros((SUBLANES, D_MODEL), jnp.float32)

    _cast_chunks((w1_src, w2_src), (w1_dst, w2_dst))

    x = x_ref[...]
    h = _rms_norm(x, g_ref[...]).astype(jnp.bfloat16)

    def proj(k, cols, hk=h):
        return _dot(hk, w_in_ref[:, k * D_MODEL + cols.start:k * D_MODEL + cols.stop])

    cw = conv_w_ref[...]
    groups = [slice(j * GROUP_COLS, (j + 1) * GROUP_COLS) for j in range(D_MODEL // GROUP_COLS)]

    gv, gv_sum, y_a = [], 0.0, []
    for cols in groups:
        gv.append(jax.nn.gelu(proj(4, cols)))
        gv_sum += jnp.sum(gv[-1], axis=-1, keepdims=True)
        z_ref[SUBLANES:SUBLANES + rows, cols] = proj(0, cols) * proj(2, cols)
        p1 = proj(1, cols)
        conv = cw[2:3, cols] * z_ref[SUBLANES:SUBLANES + rows, cols]
        conv += cw[1:2, cols] * z_ref[SUBLANES - 1:SUBLANES - 1 + rows, cols]
        conv += cw[0:1, cols] * z_ref[SUBLANES - 2:SUBLANES - 2 + rows, cols]
        y_a.append(jax.nn.sigmoid(proj(5, cols)) * (p1 * conv))
    z_ref[0:SUBLANES, :] = z_ref[rows:rows + SUBLANES, :]

    mu = gv_sum * (1.0 / D_MODEL)
    gc = [g - mu for g in gv]
    var = sum(jnp.sum(c * c, axis=-1, keepdims=True) for c in gc) * (1.0 / D_MODEL)
    rstd = lax.rsqrt(var + EPS)

    t_idx = lax.broadcasted_iota(jnp.int32, (CHUNK, CHUNK), 0)
    s_idx = lax.broadcasted_iota(jnp.int32, (CHUNK, CHUNK), 1)
    causal = s_idx <= t_idx
    bias = sgu_b_ref[...]

    def spatial_gate(vn, hd):
        n = vn.shape[0] // CHUNK
        w = jnp.where(causal, sgu_w_ref[hd], 0.0).astype(jnp.bfloat16)
        v_h = jnp.concatenate([vn[c * CHUNK:(c + 1) * CHUNK] for c in range(n)], axis=1)
        m_h = _dot(w, v_h) + bias[:, hd:hd + 1]
        return jnp.concatenate([m_h[:, c * HEAD_DIM:(c + 1) * HEAD_DIM] for c in range(n)], axis=0)

    def branch_b_merge(j, rs):
        cols = groups[j]
        gu = jax.nn.gelu(proj(3, cols, h[rs]))
        gate_b = jax.nn.sigmoid(proj(6, cols, h[rs]))
        vn = (gc[j][rs] * rstd[rs] * ln_g_ref[:, cols] + ln_b_ref[:, cols]).astype(jnp.bfloat16)
        mixed = jnp.concatenate(
            [spatial_gate(vn[:, k * HEAD_DIM:(k + 1) * HEAD_DIM], cols.start // HEAD_DIM + k)
             for k in range(GROUP_COLS // HEAD_DIM)], axis=1)
        return (y_a[j][rs] + gate_b * (gu * mixed)).astype(jnp.bfloat16)

    all_rows, halves = slice(0, rows), [slice(0, half), slice(half, rows)]
    merged = [branch_b_merge(j, all_rows) for j in range(len(groups) - 1)]
    last = [branch_b_merge(len(groups) - 1, rs) for rs in halves]
    for r, rs in enumerate(halves):
        merged_rs = jnp.concatenate([m[rs] for m in merged] + [last[r]], axis=1)
        o_ref[rs, :] = x[rs] + _dot(merged_rs, w_out_ref[...])


def _ffn_kernel(x_ref, g_ref, w1_ref, w2_ref, gf_ref, *rest, final):
    n_cast = (len(rest) - 1) // 2
    o_ref = rest[n_cast]
    _cast_chunks(rest[:n_cast], rest[n_cast + 1:])

    x = x_ref[...]
    h = _rms_norm(x, g_ref[...]).astype(jnp.bfloat16)
    acc = x
    for j in range(D_FF // FF_COLS):
        cols = slice(j * FF_COLS, (j + 1) * FF_COLS)
        hid = jnp.square(jnp.maximum(_dot(h, w1_ref[:, cols]), 0.0)).astype(jnp.bfloat16)
        acc += _dot(hid, w2_ref[cols, :])
    if final:
        acc = _rms_norm(acc, gf_ref[...])
    o_ref[...] = acc


def _resident(arr):
    zeros = (0,) * arr.ndim
    return pl.BlockSpec(arr.shape, lambda i: zeros, pipeline_mode=pl.Buffered(1))


def _layer_resident(arr, layer):
    zeros = (0,) * (arr.ndim - 1)
    return pl.BlockSpec((None,) + arr.shape[1:], lambda i: (layer,) + zeros, pipeline_mode=pl.Buffered(1))


def _cast_specs(stack, layer, steps):
    _, k, n = stack.shape
    chunk = k // steps
    assert chunk * steps == k and chunk % BF16_ROWS == 0, (stack.shape, steps)
    src = pl.BlockSpec((None, chunk, n), lambda i: (layer, i, 0))
    dst = pl.BlockSpec((chunk, n), lambda i: (i, 0))
    return src, dst, jax.ShapeDtypeStruct((k, n), jnp.bfloat16)


def _mixer(x2, layer, seq, g, w_in_b, conv_w, sgu_w, sgu_b_t, ln_g, ln_b, w_out_b, w_ff1, w_ff2):
    tokens, d = x2.shape
    rows = MIX_ROWS
    steps = tokens // rows
    xspec = pl.BlockSpec((rows, d), lambda i: (i, 0))
    stacked = lambda p: _layer_resident(p, layer)
    casts = [_cast_specs(w, layer, steps) for w in (w_ff1, w_ff2)]
    return pl.pallas_call(
        functools.partial(_mixer_kernel, blocks_per_seq=seq // rows),
        grid=(steps,),
        in_specs=[xspec, stacked(g), _resident(w_in_b), stacked(conv_w), stacked(sgu_w), stacked(sgu_b_t),
                  stacked(ln_g), stacked(ln_b), _resident(w_out_b)] + [c[0] for c in casts],
        out_specs=[xspec] + [c[1] for c in casts],
        out_shape=[jax.ShapeDtypeStruct(x2.shape, x2.dtype)] + [c[2] for c in casts],
        scratch_shapes=[pltpu.VMEM((rows + SUBLANES, d), jnp.float32)],
        compiler_params=pltpu.CompilerParams(
            dimension_semantics=("arbitrary",), vmem_limit_bytes=VMEM_LIMIT_BYTES),
        name="mixer",
    )(x2, g, w_in_b, conv_w, sgu_w, sgu_b_t, ln_g, ln_b, w_out_b, w_ff1, w_ff2)


def _ffn(x2, layer, g, w1_b, w2_b, gf, next_weights):
    tokens, d = x2.shape
    rows = FFN_ROWS
    steps = tokens // rows
    xspec = pl.BlockSpec((rows, d), lambda i: (i, 0))
    casts = [_cast_specs(w, layer + 1, steps) for w in next_weights]
    final = not next_weights
    return pl.pallas_call(
        functools.partial(_ffn_kernel, final=final),
        grid=(steps,),
        in_specs=[xspec, _layer_resident(g, layer), _resident(w1_b), _resident(w2_b), _resident(gf)]
                 + [c[0] for c in casts],
        out_specs=[xspec] + [c[1] for c in casts],
        out_shape=[jax.ShapeDtypeStruct(x2.shape, x2.dtype)] + [c[2] for c in casts],
        compiler_params=pltpu.CompilerParams(
            dimension_semantics=("arbitrary",), vmem_limit_bytes=VMEM_LIMIT_BYTES),
        name="ffn_final" if final else "ffn",
    )(x2, g, w1_b, w2_b, gf, *next_weights)


def kernel(x, norm_mix, w_in, conv_w, sgu_w, sgu_b, sgu_ln_g, sgu_ln_b, w_out, norm_mlp, w_ff1, w_ff2, final_norm):
    batch, seq, d = x.shape
    depth = w_in.shape[0]
    row = lambda p: p.reshape(depth, 1, d)
    norm_mix, norm_mlp, sgu_ln_g, sgu_ln_b = (row(p) for p in (norm_mix, norm_mlp, sgu_ln_g, sgu_ln_b))
    sgu_b_t = jnp.swapaxes(sgu_b, 1, 2)
    gf = final_norm.reshape(1, d)
    x2 = x.reshape(batch * seq, d)
    w_in_b, w_out_b = w_in[0].astype(jnp.bfloat16), w_out[0].astype(jnp.bfloat16)
    for l in range(depth):
        x2, w1_b, w2_b = _mixer(x2, l, seq, norm_mix, w_in_b, conv_w, sgu_w, sgu_b_t, sgu_ln_g, sgu_ln_b,
                                w_out_b, w_ff1, w_ff2)
        if l + 1 < depth:
            x2, w_in_b, w_out_b = _ffn(x2, l, norm_mlp, w1_b, w2_b, gf, (w_in, w_out))
        else:
            (x2,) = _ffn(x2, l, norm_mlp, w1_b, w2_b, gf, ())
    return x2.reshape(batch, seq, d)
```

```python
import functools

import jax
import jax.numpy as jnp
from jax import lax
from jax.experimental import pallas as pl
from jax.experimental.pallas import tpu as pltpu

D_MODEL = 1024
SGU_HEADS = 8
HEAD_DIM = D_MODEL // SGU_HEADS
CHUNK = 128
D_FF = 4 * D_MODEL
EPS = 1e-6

SUBLANES = 8
BF16_ROWS = 16
MIX_ROWS = 1024
FFN_ROWS = 1024
GROUP_COLS = 256
HEAD_ROWS = 256
TAIL_ROWS = 256
VMEM_LIMIT_BYTES = 56 * 1024 * 1024


def _rms_norm(x, g):
    return x * lax.rsqrt(jnp.mean(x * x, axis=-1, keepdims=True) + EPS) * g


def _dot(a, b):
    return jnp.dot(a, b, preferred_element_type=jnp.float32)


def _cast_chunks(src_refs, dst_refs):
    for src, dst in zip(src_refs, dst_refs):
        dst[...] = src[...].astype(dst.dtype)


def _mixer_kernel(x_ref, g_ref, w_in_ref, conv_w_ref, sgu_w_ref, sgu_b_ref, ln_g_ref, ln_b_ref,
                  w_out_ref, w1_src, w2_src, o_ref, w1_dst, w2_dst, z_ref, *, blocks_per_seq):
    rows = x_ref.shape[0]
    n_chunks = rows // CHUNK
    half = rows // 2

    @pl.when(pl.program_id(0) % blocks_per_seq == 0)
    def _():
        z_ref[0:SUBLANES, :] = jnp.zeros((SUBLANES, D_MODEL), jnp.float32)

    _cast_chunks((w1_src, w2_src), (w1_dst, w2_dst))

    x = x_ref[...]
    h = _rms_norm(x, g_ref[...]).astype(jnp.bfloat16)

    def proj(k, cols, hk=h):
        return _dot(hk, w_in_ref[:, k * D_MODEL + cols.start:k * D_MODEL + cols.stop])

    cw = conv_w_ref[...]
    groups = [slice(j * GROUP_COLS, (j + 1) * GROUP_COLS) for j in range(D_MODEL // GROUP_COLS)]

    def pass1(cols, rs):
        n, lo = rs.stop - rs.start, SUBLANES + rs.start
        gv_rs = jax.nn.gelu(proj(4, cols, h[rs]))
        z_ref[lo:lo + n, cols] = proj(0, cols, h[rs]) * proj(2, cols, h[rs])
        p1 = proj(1, cols, h[rs])
        conv = cw[2:3, cols] * z_ref[lo:lo + n, cols]
        conv += cw[1:2, cols] * z_ref[lo - 1:lo - 1 + n, cols]
        conv += cw[0:1, cols] * z_ref[lo - 2:lo - 2 + n, cols]
        return gv_rs, jax.nn.sigmoid(proj(5, cols, h[rs])) * (p1 * conv)

    gv, gv_sum, y_a = [], 0.0, []
    for j, cols in enumerate(groups):
        blocks = [slice(0, HEAD_ROWS), slice(HEAD_ROWS, rows)] if j == 0 else [slice(0, rows)]
        parts = [pass1(cols, rs) for rs in blocks]
        gv.append(jnp.concatenate([p[0] for p in parts], axis=0))
        y_a.append(jnp.concatenate([p[1] for p in parts], axis=0))
        gv_sum += jnp.sum(gv[-1], axis=-1, keepdims=True)
    z_ref[0:SUBLANES, :] = z_ref[rows:rows + SUBLANES, :]

    mu = gv_sum * (1.0 / D_MODEL)
    gc = [g - mu for g in gv]
    var = sum(jnp.sum(c * c, axis=-1, keepdims=True) for c in gc) * (1.0 / D_MODEL)
    rstd = lax.rsqrt(var + EPS)

    t_idx = lax.broadcasted_iota(jnp.int32, (CHUNK, CHUNK), 0)
    s_idx = lax.broadcasted_iota(jnp.int32, (CHUNK, CHUNK), 1)
    causal = s_idx <= t_idx
    bias = sgu_b_ref[...]

    def spatial_gate(vn, hd):
        n = vn.shape[0] // CHUNK
        w = jnp.where(causal, sgu_w_ref[hd], 0.0).astype(jnp.bfloat16)
        v_h = jnp.concatenate([vn[c * CHUNK:(c + 1) * CHUNK] for c in range(n)], axis=1)
        m_h = _dot(w, v_h) + bias[:, hd:hd + 1]
        return jnp.concatenate([m_h[:, c * HEAD_DIM:(c + 1) * HEAD_DIM] for c in range(n)], axis=0)

    def branch_b_merge(j, rs):
        cols = groups[j]
        gu = jax.nn.gelu(proj(3, cols, h[rs]))
        gate_b = jax.nn.sigmoid(proj(6, cols, h[rs]))
        vn = (gc[j][rs] * rstd[rs] * ln_g_ref[:, cols] + ln_b_ref[:, cols]).astype(jnp.bfloat16)
        mixed = jnp.concatenate(
            [spatial_gate(vn[:, k * HEAD_DIM:(k + 1) * HEAD_DIM], cols.start // HEAD_DIM + k)
             for k in range(GROUP_COLS // HEAD_DIM)], axis=1)
        return (y_a[j][rs] + gate_b * (gu * mixed)).astype(jnp.bfloat16)

    all_rows, halves = slice(0, rows), [slice(0, half), slice(half, rows)]
    merged = [branch_b_merge(j, all_rows) for j in range(len(groups) - 1)]
    last = [branch_b_merge(len(groups) - 1, rs) for rs in halves]
    merged = jnp.concatenate(merged + [jnp.concatenate(last, axis=0)], axis=1)
    for rs in [halves[0], slice(half, rows - TAIL_ROWS), slice(rows - TAIL_ROWS, rows)]:
        o_ref[rs, :] = x[rs] + _dot(merged[rs], w_out_ref[...])


def _ffn_kernel(x_ref, g_ref, w1_ref, w2_ref, gf_ref, *rest, final):
    n_cast = (len(rest) - 1) // 2
    o_ref = rest[n_cast]
    _cast_chunks(rest[:n_cast], rest[n_cast + 1:])

    rows = x_ref.shape[0]
    x = x_ref[...]
    h = _rms_norm(x, g_ref[...]).astype(jnp.bfloat16)
    hid = jnp.concatenate(
        [jnp.square(jnp.maximum(_dot(h[rs], w1_ref[...]), 0.0)).astype(jnp.bfloat16)
         for rs in (slice(0, HEAD_ROWS), slice(HEAD_ROWS, rows))], axis=0)
    for rs in (slice(0, rows - TAIL_ROWS), slice(rows - TAIL_ROWS, rows)):
        out = x[rs] + _dot(hid[rs], w2_ref[...])
        if final:
            out = _rms_norm(out, gf_ref[...])
        o_ref[rs, :] = out


def _resident(arr):
    zeros = (0,) * arr.ndim
    return pl.BlockSpec(arr.shape, lambda i: zeros, pipeline_mode=pl.Buffered(1))


def _layer_resident(arr, layer):
    zeros = (0,) * (arr.ndim - 1)
    return pl.BlockSpec((None,) + arr.shape[1:], lambda i: (layer,) + zeros, pipeline_mode=pl.Buffered(1))


def _cast_specs(stack, layer, steps):
    _, k, n = stack.shape
    chunk = k // steps
    assert chunk * steps == k and chunk % BF16_ROWS == 0, (stack.shape, steps)
    src = pl.BlockSpec((None, chunk, n), lambda i: (layer, i, 0))
    dst = pl.BlockSpec((chunk, n), lambda i: (i, 0))
    return src, dst, jax.ShapeDtypeStruct((k, n), jnp.bfloat16)


def _mixer(x2, layer, seq, g, w_in_b, conv_w, sgu_w, sgu_b_t, ln_g, ln_b, w_out_b, w_ff1, w_ff2):
    tokens, d = x2.shape
    rows = MIX_ROWS
    steps = tokens // rows
    xspec = pl.BlockSpec((rows, d), lambda i: (i, 0))
    stacked = lambda p: _layer_resident(p, layer)
    casts = [_cast_specs(w, layer, steps) for w in (w_ff1, w_ff2)]
    return pl.pallas_call(
        functools.partial(_mixer_kernel, blocks_per_seq=seq // rows),
        grid=(steps,),
        in_specs=[xspec, stacked(g), _resident(w_in_b), stacked(conv_w), stacked(sgu_w), stacked(sgu_b_t),
                  stacked(ln_g), stacked(ln_b), _resident(w_out_b)] + [c[0] for c in casts],
        out_specs=[xspec] + [c[1] for c in casts],
        out_shape=[jax.ShapeDtypeStruct(x2.shape, x2.dtype)] + [c[2] for c in casts],
        scratch_shapes=[pltpu.VMEM((rows + SUBLANES, d), jnp.float32)],
        compiler_params=pltpu.CompilerParams(
            dimension_semantics=("arbitrary",), vmem_limit_bytes=VMEM_LIMIT_BYTES),
        name="mixer",
    )(x2, g, w_in_b, conv_w, sgu_w, sgu_b_t, ln_g, ln_b, w_out_b, w_ff1, w_ff2)


def _ffn(x2, layer, g, w1_b, w2_b, gf, next_weights):
    tokens, d = x2.shape
    rows = FFN_ROWS
    steps = tokens // rows
    xspec = pl.BlockSpec((rows, d), lambda i: (i, 0))
    casts = [_cast_specs(w, layer + 1, steps) for w in next_weights]
    final = not next_weights
    return pl.pallas_call(
        functools.partial(_ffn_kernel, final=final),
        grid=(steps,),
        in_specs=[xspec, _layer_resident(g, layer), _resident(w1_b), _resident(w2_b), _resident(gf)]
                 + [c[0] for c in casts],
        out_specs=[xspec] + [c[1] for c in casts],
        out_shape=[jax.ShapeDtypeStruct(x2.shape, x2.dtype)] + [c[2] for c in casts],
        compiler_params=pltpu.CompilerParams(
            dimension_semantics=("arbitrary",), vmem_limit_bytes=VMEM_LIMIT_BYTES),
        name="ffn_final" if final else "ffn",
    )(x2, g, w1_b, w2_b, gf, *next_weights)


def kernel(x, norm_mix, w_in, conv_w, sgu_w, sgu_b, sgu_ln_g, sgu_ln_b, w_out, norm_mlp, w_ff1, w_ff2, final_norm):
    batch, seq, d = x.shape
    depth = w_in.shape[0]
    row = lambda p: p.reshape(depth, 1, d)
    norm_mix, norm_mlp, sgu_ln_g, sgu_ln_b = (row(p) for p in (norm_mix, norm_mlp, sgu_ln_g, sgu_ln_b))
    sgu_b_t = jnp.swapaxes(sgu_b, 1, 2)
    gf = final_norm.reshape(1, d)
    x2 = x.reshape(batch * seq, d)
    w_in_b, w_out_b = w_in[0].astype(jnp.bfloat16), w_out[0].astype(jnp.bfloat16)
    for l in range(depth):
        x2, w1_b, w2_b = _mixer(x2, l, seq, norm_mix, w_in_b, conv_w, sgu_w, sgu_b_t, sgu_ln_g, sgu_ln_b,
                                w_out_b, w_ff1, w_ff2)
        if l + 1 < depth:
            x2, w_in_b, w_out_b = _ffn(x2, l, norm_mlp, w1_b, w2_b, gf, (w_in, w_out))
        else:
            (x2,) = _ffn(x2, l, norm_mlp, w1_b, w2_b, gf, ())
    return x2.reshape(batch, seq, d)
```

```python
import functools

import jax
import jax.numpy as jnp
from jax import lax
from jax.experimental import pallas as pl
from jax.experimental.pallas import tpu as pltpu

D_MODEL = 1024
SGU_HEADS = 8
HEAD_DIM = D_MODEL // SGU_HEADS
CHUNK = 128
D_FF = 4 * D_MODEL
EPS = 1e-6

SUBLANES = 8
BF16_ROWS = 16
MIX_ROWS = 1024
FFN_ROWS = 1024
GROUP_COLS = 256
HEAD_ROWS = 256
TAIL_ROWS = 256
CAST_STEPS = 8
VMEM_LIMIT_BYTES = 56 * 1024 * 1024


def _rms_norm(x, g):
    return x * lax.rsqrt(jnp.mean(x * x, axis=-1, keepdims=True) + EPS) * g


def _dot(a, b):
    return jnp.dot(a, b, preferred_element_type=jnp.float32)


def _pack_bf16(w):
    return pltpu.bitcast(w.astype(jnp.bfloat16), jnp.uint32)


def _unpack_bf16(packed):
    return pltpu.bitcast(packed, jnp.bfloat16)


def _cast_chunks(src_refs, dst_refs):
    for src, dst in zip(src_refs, dst_refs):
        dst[...] = _pack_bf16(src[...])


def _mixer_kernel(x_ref, g_ref, w_in_ref, conv_w_ref, sgu_w_ref, sgu_b_ref, ln_g_ref, ln_b_ref,
                  w_out_ref, w1_src, w2_src, o_ref, w1_dst, w2_dst, z_ref, *, blocks_per_seq):
    rows = x_ref.shape[0]
    n_chunks = rows // CHUNK
    half = rows // 2

    @pl.when(pl.program_id(0) % blocks_per_seq == 0)
    def _():
        z_ref[0:SUBLANES, :] = jnp.zeros((SUBLANES, D_MODEL), jnp.float32)

    _cast_chunks((w1_src, w2_src), (w1_dst, w2_dst))

    x = x_ref[...]
    h = jnp.concatenate([_rms_norm(x[rs], g_ref[...]).astype(jnp.bfloat16)
                         for rs in (slice(0, HEAD_ROWS), slice(HEAD_ROWS, rows))], axis=0)

    def proj(k, cols, hk=h):
        return _dot(hk, _unpack_bf16(w_in_ref[:, k * D_MODEL + cols.start:k * D_MODEL + cols.stop]))

    cw = conv_w_ref[...]
    groups = [slice(j * GROUP_COLS, (j + 1) * GROUP_COLS) for j in range(D_MODEL // GROUP_COLS)]

    def pass1(cols, rs):
        n, lo = rs.stop - rs.start, SUBLANES + rs.start
        gv_rs = jax.nn.gelu(proj(4, cols, h[rs]))
        z_ref[lo:lo + n, cols] = proj(0, cols, h[rs]) * proj(2, cols, h[rs])
        p1 = proj(1, cols, h[rs])
        conv = cw[2:3, cols] * z_ref[lo:lo + n, cols]
        conv += cw[1:2, cols] * z_ref[lo - 1:lo - 1 + n, cols]
        conv += cw[0:1, cols] * z_ref[lo - 2:lo - 2 + n, cols]
        return gv_rs, jax.nn.sigmoid(proj(5, cols, h[rs])) * (p1 * conv)

    gv, gv_sum, y_a = [], 0.0, []
    for j, cols in enumerate(groups):
        blocks = [slice(0, HEAD_ROWS), slice(HEAD_ROWS, rows)] if j == 0 else [slice(0, rows)]
        parts = [pass1(cols, rs) for rs in blocks]
        gv.append(jnp.concatenate([p[0] for p in parts], axis=0))
        y_a.append(jnp.concatenate([p[1] for p in parts], axis=0))
        gv_sum += jnp.sum(gv[-1], axis=-1, keepdims=True)
    z_ref[0:SUBLANES, :] = z_ref[rows:rows + SUBLANES, :]

    mu = gv_sum * (1.0 / D_MODEL)
    gc = [g - mu for g in gv]
    var = sum(jnp.sum(c * c, axis=-1, keepdims=True) for c in gc) * (1.0 / D_MODEL)
    rstd = lax.rsqrt(var + EPS)

    t_idx = lax.broadcasted_iota(jnp.int32, (CHUNK, CHUNK), 0)
    s_idx = lax.broadcasted_iota(jnp.int32, (CHUNK, CHUNK), 1)
    causal = s_idx <= t_idx
    bias = sgu_b_ref[...]

    def spatial_gate(vn, hd):
        n = vn.shape[0] // CHUNK
        w = jnp.where(causal, sgu_w_ref[hd], 0.0).astype(jnp.bfloat16)
        v_h = jnp.concatenate([vn[c * CHUNK:(c + 1) * CHUNK] for c in range(n)], axis=1)
        m_h = _dot(w, v_h) + bias[:, hd:hd + 1]
        return jnp.concatenate([m_h[:, c * HEAD_DIM:(c + 1) * HEAD_DIM] for c in range(n)], axis=0)

    def branch_b_merge(j, rs):
        cols = groups[j]
        gu = jax.nn.gelu(proj(3, cols, h[rs]))
        gate_b = jax.nn.sigmoid(proj(6, cols, h[rs]))
        vn = (gc[j][rs] * rstd[rs] * ln_g_ref[:, cols] + ln_b_ref[:, cols]).astype(jnp.bfloat16)
        mixed = jnp.concatenate(
            [spatial_gate(vn[:, k * HEAD_DIM:(k + 1) * HEAD_DIM], cols.start // HEAD_DIM + k)
             for k in range(GROUP_COLS // HEAD_DIM)], axis=1)
        return (y_a[j][rs] + gate_b * (gu * mixed)).astype(jnp.bfloat16)

    all_rows, halves = slice(0, rows), [slice(0, half), slice(half, rows)]
    merged = [branch_b_merge(j, all_rows) for j in range(len(groups) - 1)]
    last = [branch_b_merge(len(groups) - 1, rs) for rs in halves]
    merged = jnp.concatenate(merged + [jnp.concatenate(last, axis=0)], axis=1)
    for rs in [halves[0], slice(half, rows - TAIL_ROWS), slice(rows - TAIL_ROWS, rows)]:
        o_ref[rs, :] = x[rs] + _dot(merged[rs], _unpack_bf16(w_out_ref[...]))


def _ffn_kernel(x_ref, g_ref, w1_ref, w2_ref, gf_ref, *rest, final):
    n_cast = (len(rest) - 1) // 2
    o_ref = rest[n_cast]
    _cast_chunks(rest[:n_cast], rest[n_cast + 1:])

    rows = x_ref.shape[0]
    x = x_ref[...]
    h = jnp.concatenate([_rms_norm(x[rs], g_ref[...]).astype(jnp.bfloat16)
                         for rs in (slice(0, HEAD_ROWS), slice(HEAD_ROWS, rows))], axis=0)
    hid = jnp.concatenate(
        [jnp.square(jnp.maximum(_dot(h[rs], _unpack_bf16(w1_ref[...])), 0.0)).astype(jnp.bfloat16)
         for rs in (slice(0, HEAD_ROWS), slice(HEAD_ROWS, rows))], axis=0)
    bounds = range(0, rows + 1, TAIL_ROWS) if final else (0, rows - TAIL_ROWS, rows)
    for rs in (slice(a, b) for a, b in zip(bounds[:-1], bounds[1:])):
        out = x[rs] + _dot(hid[rs], _unpack_bf16(w2_ref[...]))
        if final:
            out = _rms_norm(out, gf_ref[...])
        o_ref[rs, :] = out


def _resident(arr):
    zeros = (0,) * arr.ndim
    return pl.BlockSpec(arr.shape, lambda i: zeros, pipeline_mode=pl.Buffered(1))


def _layer_resident(arr, layer):
    zeros = (0,) * (arr.ndim - 1)
    return pl.BlockSpec((None,) + arr.shape[1:], lambda i: (layer,) + zeros, pipeline_mode=pl.Buffered(1))


def _cast_specs(stack, layer, steps):
    _, k, n = stack.shape
    chunk = k // steps
    assert chunk * steps == k and chunk % BF16_ROWS == 0, (stack.shape, steps)
    src = pl.BlockSpec((None, chunk, n), lambda i: (layer, i, 0))
    dst = pl.BlockSpec((chunk // 2, n), lambda i: (i, 0))
    return src, dst, jax.ShapeDtypeStruct((k // 2, n), jnp.uint32)


def _cast_kernel(*refs):
    _cast_chunks(refs[:len(refs) // 2], refs[len(refs) // 2:])


def _cast_weights(stacks, layer):
    casts = [_cast_specs(w, layer, CAST_STEPS) for w in stacks]
    return pl.pallas_call(
        _cast_kernel,
        grid=(CAST_STEPS,),
        in_specs=[c[0] for c in casts],
        out_specs=[c[1] for c in casts],
        out_shape=[c[2] for c in casts],
        compiler_params=pltpu.CompilerParams(dimension_semantics=("arbitrary",)),
        name="cast",
    )(*stacks)


def _mixer(x2, layer, seq, g, w_in_b, conv_w, sgu_w, sgu_b_t, ln_g, ln_b, w_out_b, w_ff1, w_ff2):
    tokens, d = x2.shape
    rows = MIX_ROWS
    steps = tokens // rows
    xspec = pl.BlockSpec((rows, d), lambda i: (i, 0))
    stacked = lambda p: _layer_resident(p, layer)
    casts = [_cast_specs(w, layer, steps) for w in (w_ff1, w_ff2)]
    return pl.pallas_call(
        functools.partial(_mixer_kernel, blocks_per_seq=seq // rows),
        grid=(steps,),
        in_specs=[xspec, stacked(g), _resident(w_in_b), stacked(conv_w), stacked(sgu_w), stacked(sgu_b_t),
                  stacked(ln_g), stacked(ln_b), _resident(w_out_b)] + [c[0] for c in casts],
        out_specs=[xspec] + [c[1] for c in casts],
        out_shape=[jax.ShapeDtypeStruct(x2.shape, x2.dtype)] + [c[2] for c in casts],
        scratch_shapes=[pltpu.VMEM((rows + SUBLANES, d), jnp.float32)],
        compiler_params=pltpu.CompilerParams(
            dimension_semantics=("arbitrary",), vmem_limit_bytes=VMEM_LIMIT_BYTES),
        name="mixer",
    )(x2, g, w_in_b, conv_w, sgu_w, sgu_b_t, ln_g, ln_b, w_out_b, w_ff1, w_ff2)


def _ffn(x2, layer, g, w1_b, w2_b, gf, next_weights):
    tokens, d = x2.shape
    rows = FFN_ROWS
    steps = tokens // rows
    xspec = pl.BlockSpec((rows, d), lambda i: (i, 0))
    casts = [_cast_specs(w, layer + 1, steps) for w in next_weights]
    final = not next_weights
    return pl.pallas_call(
        functools.partial(_ffn_kernel, final=final),
        grid=(steps,),
        in_specs=[xspec, _layer_resident(g, layer), _resident(w1_b), _resident(w2_b), _resident(gf)]
                 + [c[0] for c in casts],
        out_specs=[xspec] + [c[1] for c in casts],
        out_shape=[jax.ShapeDtypeStruct(x2.shape, x2.dtype)] + [c[2] for c in casts],
        compiler_params=pltpu.CompilerParams(
            dimension_semantics=("arbitrary",), vmem_limit_bytes=VMEM_LIMIT_BYTES),
        name="ffn_final" if final else "ffn",
    )(x2, g, w1_b, w2_b, gf, *next_weights)


def kernel(x, norm_mix, w_in, conv_w, sgu_w, sgu_b, sgu_ln_g, sgu_ln_b, w_out, norm_mlp, w_ff1, w_ff2, final_norm):
    batch, seq, d = x.shape
    depth = w_in.shape[0]
    row = lambda p: p.reshape(depth, 1, d)
    norm_mix, norm_mlp, sgu_ln_g, sgu_ln_b = (row(p) for p in (norm_mix, norm_mlp, sgu_ln_g, sgu_ln_b))
    sgu_b_t = jnp.swapaxes(sgu_b, 1, 2)
    gf = final_norm.reshape(1, d)
    x2 = x.reshape(batch * seq, d)
    w_in_b, w_out_b = _cast_weights((w_in, w_out), 0)
    for l in range(depth):
        x2, w1_b, w2_b = _mixer(x2, l, seq, norm_mix, w_in_b, conv_w, sgu_w, sgu_b_t, sgu_ln_g, sgu_ln_b,
                                w_out_b, w_ff1, w_ff2)
        if l + 1 < depth:
            x2, w_in_b, w_out_b = _ffn(x2, l, norm_mlp, w1_b, w2_b, gf, (w_in, w_out))
        else:
            (x2,) = _ffn(x2, l, norm_mlp, w1_b, w2_b, gf, ())
    return x2.reshape(batch, seq, d)
```

```python
import functools

import jax
import jax.numpy as jnp
from jax import lax
from jax.experimental import pallas as pl
from jax.experimental.pallas import tpu as pltpu

D_MODEL = 1024
SGU_HEADS = 8
HEAD_DIM = D_MODEL // SGU_HEADS
CHUNK = 128
D_FF = 4 * D_MODEL
EPS = 1e-6

SUBLANES = 8
BF16_ROWS = 16
MIX_ROWS = 1024
FFN_ROWS = 1024
GROUP_COLS = 256
HEAD_ROWS = 256
TAIL_ROWS = 256
CAST_STEPS = 8
VMEM_LIMIT_BYTES = 60 * 1024 * 1024


def _rms_norm(x, g):
    return x * lax.rsqrt(jnp.mean(x * x, axis=-1, keepdims=True) + EPS) * g


def _dot(a, b):
    return jnp.dot(a, b, preferred_element_type=jnp.float32)


def _pack_bf16(w):
    return pltpu.bitcast(w.astype(jnp.bfloat16), jnp.uint32)


def _unpack_bf16(packed):
    return pltpu.bitcast(packed, jnp.bfloat16)


def _cast_chunks(src_refs, dst_refs):
    for src, dst in zip(src_refs, dst_refs):
        dst[...] = _pack_bf16(src[...])


def _mixer_kernel(x_ref, h_src_ref, w_in_ref, conv_w_ref, sgu_w_ref, sgu_b_ref, ln_g_ref, ln_b_ref,
                  w_out_ref, w1_src, w2_src, o_ref, w1_dst, w2_dst, z_ref, *, blocks_per_seq, h_given):
    rows = x_ref.shape[0]
    n_chunks = rows // CHUNK
    half = rows // 2

    @pl.when(pl.program_id(0) % blocks_per_seq == 0)
    def _():
        z_ref[0:SUBLANES, :] = jnp.zeros((SUBLANES, D_MODEL), jnp.float32)

    _cast_chunks((w1_src, w2_src), (w1_dst, w2_dst))

    x = x_ref[...]
    head_blocks = [slice(0, HEAD_ROWS), slice(HEAD_ROWS, rows)]
    if h_given:
        h = h_src_ref[...]
    else:
        h = jnp.concatenate([_rms_norm(x[rs], h_src_ref[...]).astype(jnp.bfloat16) for rs in head_blocks], axis=0)

    def proj(k, cols, hk=h):
        return _dot(hk, _unpack_bf16(w_in_ref[:, k * D_MODEL + cols.start:k * D_MODEL + cols.stop]))

    cw = conv_w_ref[...]
    groups = [slice(j * GROUP_COLS, (j + 1) * GROUP_COLS) for j in range(D_MODEL // GROUP_COLS)]

    def pass1(cols, rs):
        n, lo = rs.stop - rs.start, SUBLANES + rs.start
        gv_rs = jax.nn.gelu(proj(4, cols, h[rs]))
        z_ref[lo:lo + n, cols] = proj(0, cols, h[rs]) * proj(2, cols, h[rs])
        p1 = proj(1, cols, h[rs])
        conv = cw[2:3, cols] * z_ref[lo:lo + n, cols]
        conv += cw[1:2, cols] * z_ref[lo - 1:lo - 1 + n, cols]
        conv += cw[0:1, cols] * z_ref[lo - 2:lo - 2 + n, cols]
        return gv_rs, jax.nn.sigmoid(proj(5, cols, h[rs])) * (p1 * conv)

    gv, gv_sum, y_a = [], 0.0, []
    for j, cols in enumerate(groups):
        blocks = head_blocks if j == 0 else [slice(0, rows)]
        parts = [pass1(cols, rs) for rs in blocks]
        gv.append(jnp.concatenate([p[0] for p in parts], axis=0))
        y_a.append(jnp.concatenate([p[1] for p in parts], axis=0))
        gv_sum += jnp.sum(gv[-1], axis=-1, keepdims=True)
    z_ref[0:SUBLANES, :] = z_ref[rows:rows + SUBLANES, :]

    mu = gv_sum * (1.0 / D_MODEL)
    gc = [g - mu for g in gv]
    var = sum(jnp.sum(c * c, axis=-1, keepdims=True) for c in gc) * (1.0 / D_MODEL)
    rstd = lax.rsqrt(var + EPS)

    t_idx = lax.broadcasted_iota(jnp.int32, (CHUNK, CHUNK), 0)
    s_idx = lax.broadcasted_iota(jnp.int32, (CHUNK, CHUNK), 1)
    causal = s_idx <= t_idx
    bias = sgu_b_ref[...]

    def spatial_gate(vn, hd):
        n = vn.shape[0] // CHUNK
        w = jnp.where(causal, sgu_w_ref[hd], 0.0).astype(jnp.bfloat16)
        v_h = jnp.concatenate([vn[c * CHUNK:(c + 1) * CHUNK] for c in range(n)], axis=1)
        m_h = _dot(w, v_h) + bias[:, hd:hd + 1]
        return jnp.concatenate([m_h[:, c * HEAD_DIM:(c + 1) * HEAD_DIM] for c in range(n)], axis=0)

    def branch_b_merge(j, rs):
        cols = groups[j]
        gu = jax.nn.gelu(proj(3, cols, h[rs]))
        gate_b = jax.nn.sigmoid(proj(6, cols, h[rs]))
        vn = (gc[j][rs] * rstd[rs] * ln_g_ref[:, cols] + ln_b_ref[:, cols]).astype(jnp.bfloat16)
        mixed = jnp.concatenate(
            [spatial_gate(vn[:, k * HEAD_DIM:(k + 1) * HEAD_DIM], cols.start // HEAD_DIM + k)
             for k in range(GROUP_COLS // HEAD_DIM)], axis=1)
        return (y_a[j][rs] + gate_b * (gu * mixed)).astype(jnp.bfloat16)

    all_rows, halves = slice(0, rows), [slice(0, half), slice(half, rows)]
    merged = [branch_b_merge(j, all_rows) for j in range(len(groups) - 1)]
    last = [branch_b_merge(len(groups) - 1, rs) for rs in halves]
    merged = jnp.concatenate(merged + [jnp.concatenate(last, axis=0)], axis=1)
    for rs in [halves[0], slice(half, rows - TAIL_ROWS), slice(rows - TAIL_ROWS, rows)]:
        o_ref[rs, :] = x[rs] + _dot(merged[rs], _unpack_bf16(w_out_ref[...]))


def _ffn_kernel(x_ref, g_ref, w1_ref, w2_ref, gn_ref, *rest, final):
    n_cast = len(rest) // 2 - (0 if final else 1)
    o_ref = rest[n_cast]
    _cast_chunks(rest[:n_cast], rest[len(rest) - n_cast:])

    rows = x_ref.shape[0]
    x = x_ref[...]
    h = jnp.concatenate([_rms_norm(x[rs], g_ref[...]).astype(jnp.bfloat16)
                         for rs in (slice(0, HEAD_ROWS), slice(HEAD_ROWS, rows))], axis=0)
    hid = jnp.concatenate(
        [jnp.square(jnp.maximum(_dot(h[rs], _unpack_bf16(w1_ref[...])), 0.0)).astype(jnp.bfloat16)
         for rs in (slice(0, HEAD_ROWS), slice(HEAD_ROWS, rows))], axis=0)
    for rs in (slice(a, a + TAIL_ROWS) for a in range(0, rows, TAIL_ROWS)):
        out = x[rs] + _dot(hid[rs], _unpack_bf16(w2_ref[...]))
        normed = _rms_norm(out, gn_ref[...])
        if final:
            o_ref[rs, :] = normed
        else:
            o_ref[rs, :] = out
            rest[n_cast + 1][rs, :] = normed.astype(jnp.bfloat16)


def _resident(arr):
    zeros = (0,) * arr.ndim
    return pl.BlockSpec(arr.shape, lambda i: zeros, pipeline_mode=pl.Buffered(1))


def _layer_resident(arr, layer):
    zeros = (0,) * (arr.ndim - 1)
    return pl.BlockSpec((None,) + arr.shape[1:], lambda i: (layer,) + zeros, pipeline_mode=pl.Buffered(1))


def _cast_specs(stack, layer, steps):
    _, k, n = stack.shape
    chunk = k // steps
    assert chunk * steps == k and chunk % BF16_ROWS == 0, (stack.shape, steps)
    src = pl.BlockSpec((None, chunk, n), lambda i: (layer, i, 0))
    dst = pl.BlockSpec((chunk // 2, n), lambda i: (i, 0))
    return src, dst, jax.ShapeDtypeStruct((k // 2, n), jnp.uint32)


def _cast_kernel(*refs):
    _cast_chunks(refs[:len(refs) // 2], refs[len(refs) // 2:])


def _cast_weights(stacks, layer):
    casts = [_cast_specs(w, layer, CAST_STEPS) for w in stacks]
    return pl.pallas_call(
        _cast_kernel,
        grid=(CAST_STEPS,),
        in_specs=[c[0] for c in casts],
        out_specs=[c[1] for c in casts],
        out_shape=[c[2] for c in casts],
        compiler_params=pltpu.CompilerParams(dimension_semantics=("arbitrary",)),
        name="cast",
    )(*stacks)


def _mixer(x2, h2, layer, seq, g, w_in_b, conv_w, sgu_w, sgu_b_t, ln_g, ln_b, w_out_b, w_ff1, w_ff2):
    tokens, d = x2.shape
    rows = MIX_ROWS
    steps = tokens // rows
    xspec = pl.BlockSpec((rows, d), lambda i: (i, 0))
    stacked = lambda p: _layer_resident(p, layer)
    casts = [_cast_specs(w, layer, steps) for w in (w_ff1, w_ff2)]
    h_src, h_src_spec = (g, stacked(g)) if h2 is None else (h2, xspec)
    return pl.pallas_call(
        functools.partial(_mixer_kernel, blocks_per_seq=seq // rows, h_given=h2 is not None),
        grid=(steps,),
        in_specs=[xspec, h_src_spec, _resident(w_in_b), stacked(conv_w), stacked(sgu_w), stacked(sgu_b_t),
                  stacked(ln_g), stacked(ln_b), _resident(w_out_b)] + [c[0] for c in casts],
        out_specs=[xspec] + [c[1] for c in casts],
        out_shape=[jax.ShapeDtypeStruct(x2.shape, x2.dtype)] + [c[2] for c in casts],
        scratch_shapes=[pltpu.VMEM((rows + SUBLANES, d), jnp.float32)],
        compiler_params=pltpu.CompilerParams(
            dimension_semantics=("arbitrary",), vmem_limit_bytes=VMEM_LIMIT_BYTES),
        name="mixer",
    )(x2, h_src, w_in_b, conv_w, sgu_w, sgu_b_t, ln_g, ln_b, w_out_b, w_ff1, w_ff2)


def _ffn(x2, layer, g, w1_b, w2_b, g_next, next_weights):
    tokens, d = x2.shape
    rows = FFN_ROWS
    steps = tokens // rows
    xspec = pl.BlockSpec((rows, d), lambda i: (i, 0))
    casts = [_cast_specs(w, layer + 1, steps) for w in next_weights]
    final = not next_weights
    gn_spec = _resident(g_next) if final else _layer_resident(g_next, layer + 1)
    h_out = [] if final else [(xspec, jax.ShapeDtypeStruct(x2.shape, jnp.bfloat16))]
    return pl.pallas_call(
        functools.partial(_ffn_kernel, final=final),
        grid=(steps,),
        in_specs=[xspec, _layer_resident(g, layer), _resident(w1_b), _resident(w2_b), gn_spec]
                 + [c[0] for c in casts],
        out_specs=[xspec] + [o[0] for o in h_out] + [c[1] for c in casts],
        out_shape=[jax.ShapeDtypeStruct(x2.shape, x2.dtype)] + [o[1] for o in h_out] + [c[2] for c in casts],
        compiler_params=pltpu.CompilerParams(
            dimension_semantics=("arbitrary",), vmem_limit_bytes=VMEM_LIMIT_BYTES),
        name="ffn_final" if final else "ffn",
    )(x2, g, w1_b, w2_b, g_next, *next_weights)


def kernel(x, norm_mix, w_in, conv_w, sgu_w, sgu_b, sgu_ln_g, sgu_ln_b, w_out, norm_mlp, w_ff1, w_ff2, final_norm):
    batch, seq, d = x.shape
    depth = w_in.shape[0]
    row = lambda p: p.reshape(depth, 1, d)
    norm_mix, norm_mlp, sgu_ln_g, sgu_ln_b = (row(p) for p in (norm_mix, norm_mlp, sgu_ln_g, sgu_ln_b))
    sgu_b_t = jnp.swapaxes(sgu_b, 1, 2)
    gf = final_norm.reshape(1, d)
    x2 = x.reshape(batch * seq, d)
    w_in_b, w_out_b = _cast_weights((w_in, w_out), 0)
    h2 = None
    for l in range(depth):
        x2, w1_b, w2_b = _mixer(x2, h2, l, seq, norm_mix, w_in_b, conv_w, sgu_w, sgu_b_t, sgu_ln_g, sgu_ln_b,
                                w_out_b, w_ff1, w_ff2)
        if l + 1 < depth:
            x2, h2, w_in_b, w_out_b = _ffn(x2, l, norm_mlp, w1_b, w2_b, norm_mix, (w_in, w_out))
        else:
            (x2,) = _ffn(x2, l, norm_mlp, w1_b, w2_b, gf, ())
    return x2.reshape(batch, seq, d)
```

```python
import functools

import jax
import jax.numpy as jnp
from jax import lax
from jax.experimental import pallas as pl
from jax.experimental.pallas import tpu as pltpu

D_MODEL = 1024
SGU_HEADS = 8
HEAD_DIM = D_MODEL // SGU_HEADS
CHUNK = 128
D_FF = 4 * D_MODEL
EPS = 1e-6

SUBLANES = 8
BF16_ROWS = 16
MIX_ROWS = 1024
FFN_ROWS = 1024
GROUP_COLS = 256
HEAD_ROWS = 256
TAIL_ROWS = 256
CAST_STEPS = 8
VMEM_LIMIT_BYTES = 60 * 1024 * 1024


def _rms_norm(x, g):
    return x * lax.rsqrt(jnp.mean(x * x, axis=-1, keepdims=True) + EPS) * g


def _dot(a, b):
    return jnp.dot(a, b, preferred_element_type=jnp.float32)


def _pack_bf16(w):
    return pltpu.bitcast(w.astype(jnp.bfloat16), jnp.uint32)


def _unpack_bf16(packed):
    return pltpu.bitcast(packed, jnp.bfloat16)


def _cast_chunks(src_refs, dst_refs):
    for src, dst in zip(src_refs, dst_refs):
        dst[...] = _pack_bf16(src[...])


def _mixer_kernel(x_ref, h_src_ref, w_in_ref, conv_w_ref, sgu_w_ref, sgu_b_ref, ln_g_ref, ln_b_ref,
                  w_out_ref, w1_src, w2_src, o_ref, w1_dst, w2_dst, z_ref, *, blocks_per_seq, h_given):
    rows = x_ref.shape[0]
    half = rows // 2

    @pl.when(pl.program_id(0) % blocks_per_seq == 0)
    def _():
        z_ref[0:SUBLANES, :] = jnp.zeros((SUBLANES, D_MODEL), jnp.float32)

    _cast_chunks((w1_src, w2_src), (w1_dst, w2_dst))

    x = x_ref[...]
    head_blocks = [slice(0, HEAD_ROWS), slice(HEAD_ROWS, rows)]
    if h_given:
        h = h_src_ref[...]
    else:
        h = jnp.concatenate([_rms_norm(x[rs], h_src_ref[...]).astype(jnp.bfloat16) for rs in head_blocks], axis=0)

    def proj(k, cols, hk=h):
        return _dot(hk, _unpack_bf16(w_in_ref[:, k * D_MODEL + cols.start:k * D_MODEL + cols.stop]))

    cw = conv_w_ref[...]
    groups = [slice(j * GROUP_COLS, (j + 1) * GROUP_COLS) for j in range(D_MODEL // GROUP_COLS)]

    def pass1(cols, rs):
        n, lo = rs.stop - rs.start, SUBLANES + rs.start
        gv_rs = jax.nn.gelu(proj(4, cols, h[rs]))
        z_ref[lo:lo + n, cols] = proj(0, cols, h[rs]) * proj(2, cols, h[rs])
        p1 = proj(1, cols, h[rs])
        conv = cw[2:3, cols] * z_ref[lo:lo + n, cols]
        conv += cw[1:2, cols] * z_ref[lo - 1:lo - 1 + n, cols]
        conv += cw[0:1, cols] * z_ref[lo - 2:lo - 2 + n, cols]
        return gv_rs, jax.nn.sigmoid(proj(5, cols, h[rs])) * (p1 * conv)

    gv, gv_sum, y_a = [], 0.0, []
    for j, cols in enumerate(groups):
        blocks = head_blocks if j == 0 else [slice(0, rows)]
        parts = [pass1(cols, rs) for rs in blocks]
        gv.append(jnp.concatenate([p[0] for p in parts], axis=0))
        y_a.append(jnp.concatenate([p[1] for p in parts], axis=0))
        gv_sum += jnp.sum(gv[-1], axis=-1, keepdims=True)
    z_ref[0:SUBLANES, :] = z_ref[rows:rows + SUBLANES, :]

    mu = gv_sum * (1.0 / D_MODEL)
    gc = [g - mu for g in gv]
    var = sum(jnp.sum(c * c, axis=-1, keepdims=True) for c in gc) * (1.0 / D_MODEL)
    rstd = lax.rsqrt(var + EPS)

    t_idx = lax.broadcasted_iota(jnp.int32, (CHUNK, CHUNK), 0)
    s_idx = lax.broadcasted_iota(jnp.int32, (CHUNK, CHUNK), 1)
    causal = s_idx <= t_idx
    bias = sgu_b_ref[...]

    def spatial_gate(vn, hd):
        n = vn.shape[0] // CHUNK
        w = jnp.where(causal, sgu_w_ref[hd], 0.0).astype(jnp.bfloat16)
        v_h = jnp.concatenate([vn[c * CHUNK:(c + 1) * CHUNK] for c in range(n)], axis=1)
        m_h = _dot(w, v_h) + bias[:, hd:hd + 1]
        return jnp.concatenate([m_h[:, c * HEAD_DIM:(c + 1) * HEAD_DIM] for c in range(n)], axis=0)

    def branch_b_merge(j, rs):
        cols = groups[j]
        gu = jax.nn.gelu(proj(3, cols, h[rs]))
        vn = (gc[j][rs] * rstd[rs] * ln_g_ref[:, cols] + ln_b_ref[:, cols]).astype(jnp.bfloat16)
        mixed = jnp.concatenate(
            [spatial_gate(vn[:, k * HEAD_DIM:(k + 1) * HEAD_DIM], cols.start // HEAD_DIM + k)
             for k in range(GROUP_COLS // HEAD_DIM)], axis=1)
        gate_b = jax.nn.sigmoid(proj(6, cols, h[rs]))
        return (y_a[j][rs] + gate_b * (gu * mixed)).astype(jnp.bfloat16)

    all_rows, halves = slice(0, rows), [slice(0, half), slice(half, rows)]
    merged = [branch_b_merge(j, all_rows) for j in range(len(groups) - 1)]
    last = [branch_b_merge(len(groups) - 1, rs) for rs in halves]
    merged = jnp.concatenate(merged + [jnp.concatenate(last, axis=0)], axis=1)
    for rs in [halves[0], slice(half, rows - TAIL_ROWS), slice(rows - TAIL_ROWS, rows)]:
        o_ref[rs, :] = x[rs] + _dot(merged[rs], _unpack_bf16(w_out_ref[...]))


def _ffn_kernel(x_ref, g_ref, w1_ref, w2_ref, gn_ref, *rest, final):
    n_cast = len(rest) // 2 - (0 if final else 1)
    o_ref = rest[n_cast]
    _cast_chunks(rest[:n_cast], rest[len(rest) - n_cast:])

    rows = x_ref.shape[0]
    x = x_ref[...]
    h = jnp.concatenate([_rms_norm(x[rs], g_ref[...]).astype(jnp.bfloat16)
                         for rs in (slice(0, HEAD_ROWS), slice(HEAD_ROWS, rows))], axis=0)
    hid = jnp.concatenate(
        [jnp.square(jnp.maximum(_dot(h[rs], _unpack_bf16(w1_ref[...])), 0.0)).astype(jnp.bfloat16)
         for rs in (slice(0, HEAD_ROWS), slice(HEAD_ROWS, rows))], axis=0)
    for rs in (slice(a, a + TAIL_ROWS) for a in range(0, rows, TAIL_ROWS)):
        out = x[rs] + _dot(hid[rs], _unpack_bf16(w2_ref[...]))
        normed = _rms_norm(out, gn_ref[...])
        if final:
            o_ref[rs, :] = normed
        else:
            o_ref[rs, :] = out
            rest[n_cast + 1][rs, :] = normed.astype(jnp.bfloat16)


def _resident(arr):
    zeros = (0,) * arr.ndim
    return pl.BlockSpec(arr.shape, lambda i: zeros, pipeline_mode=pl.Buffered(1))


def _layer_resident(arr, layer):
    zeros = (0,) * (arr.ndim - 1)
    return pl.BlockSpec((None,) + arr.shape[1:], lambda i: (layer,) + zeros, pipeline_mode=pl.Buffered(1))


def _cast_specs(stack, layer, steps):
    _, k, n = stack.shape
    chunk = k // steps
    assert chunk * steps == k and chunk % BF16_ROWS == 0, (stack.shape, steps)
    src = pl.BlockSpec((None, chunk, n), lambda i: (layer, i, 0))
    dst = pl.BlockSpec((chunk // 2, n), lambda i: (i, 0))
    return src, dst, jax.ShapeDtypeStruct((k // 2, n), jnp.uint32)


def _cast_kernel(*refs):
    _cast_chunks(refs[:len(refs) // 2], refs[len(refs) // 2:])


def _cast_weights(stacks, layer):
    casts = [_cast_specs(w, layer, CAST_STEPS) for w in stacks]
    return pl.pallas_call(
        _cast_kernel,
        grid=(CAST_STEPS,),
        in_specs=[c[0] for c in casts],
        out_specs=[c[1] for c in casts],
        out_shape=[c[2] for c in casts],
        compiler_params=pltpu.CompilerParams(dimension_semantics=("arbitrary",)),
        name="cast",
    )(*stacks)


def _mixer(x2, h2, layer, seq, g, w_in_b, conv_w, sgu_w, sgu_b_t, ln_g, ln_b, w_out_b, w_ff1, w_ff2):
    tokens, d = x2.shape
    rows = MIX_ROWS
    steps = tokens // rows
    xspec = pl.BlockSpec((rows, d), lambda i: (i, 0))
    stacked = lambda p: _layer_resident(p, layer)
    casts = [_cast_specs(w, layer, steps) for w in (w_ff1, w_ff2)]
    h_src, h_src_spec = (g, stacked(g)) if h2 is None else (h2, xspec)
    return pl.pallas_call(
        functools.partial(_mixer_kernel, blocks_per_seq=seq // rows, h_given=h2 is not None),
        grid=(steps,),
        in_specs=[xspec, h_src_spec, _resident(w_in_b), stacked(conv_w), stacked(sgu_w), stacked(sgu_b_t),
                  stacked(ln_g), stacked(ln_b), _resident(w_out_b)] + [c[0] for c in casts],
        out_specs=[xspec] + [c[1] for c in casts],
        out_shape=[jax.ShapeDtypeStruct(x2.shape, x2.dtype)] + [c[2] for c in casts],
        scratch_shapes=[pltpu.VMEM((rows + SUBLANES, d), jnp.float32)],
        compiler_params=pltpu.CompilerParams(
            dimension_semantics=("arbitrary",), vmem_limit_bytes=VMEM_LIMIT_BYTES),
        name="mixer",
    )(x2, h_src, w_in_b, conv_w, sgu_w, sgu_b_t, ln_g, ln_b, w_out_b, w_ff1, w_ff2)


def _ffn(x2, layer, g, w1_b, w2_b, g_next, next_weights):
    tokens, d = x2.shape
    rows = FFN_ROWS
    steps = tokens // rows
    xspec = pl.BlockSpec((rows, d), lambda i: (i, 0))
    casts = [_cast_specs(w, layer + 1, steps) for w in next_weights]
    final = not next_weights
    gn_spec = _resident(g_next) if final else _layer_resident(g_next, layer + 1)
    h_out = [] if final else [(xspec, jax.ShapeDtypeStruct(x2.shape, jnp.bfloat16))]
    return pl.pallas_call(
        functools.partial(_ffn_kernel, final=final),
        grid=(steps,),
        in_specs=[xspec, _layer_resident(g, layer), _resident(w1_b), _resident(w2_b), gn_spec]
                 + [c[0] for c in casts],
        out_specs=[xspec] + [o[0] for o in h_out] + [c[1] for c in casts],
        out_shape=[jax.ShapeDtypeStruct(x2.shape, x2.dtype)] + [o[1] for o in h_out] + [c[2] for c in casts],
        compiler_params=pltpu.CompilerParams(
            dimension_semantics=("arbitrary",), vmem_limit_bytes=VMEM_LIMIT_BYTES),
        name="ffn_final" if final else "ffn",
    )(x2, g, w1_b, w2_b, g_next, *next_weights)


def kernel(x, norm_mix, w_in, conv_w, sgu_w, sgu_b, sgu_ln_g, sgu_ln_b, w_out, norm_mlp, w_ff1, w_ff2, final_norm):
    batch, seq, d = x.shape
    depth = w_in.shape[0]
    row = lambda p: p.reshape(depth, 1, d)
    norm_mix, norm_mlp, sgu_ln_g, sgu_ln_b = (row(p) for p in (norm_mix, norm_mlp, sgu_ln_g, sgu_ln_b))
    sgu_b_t = jnp.swapaxes(sgu_b, 1, 2)
    gf = final_norm.reshape(1, d)
    x2 = x.reshape(batch * seq, d)
    w_in_b, w_out_b = _cast_weights((w_in, w_out), 0)
    h2 = None
    for l in range(depth):
        x2, w1_b, w2_b = _mixer(x2, h2, l, seq, norm_mix, w_in_b, conv_w, sgu_w, sgu_b_t, sgu_ln_g, sgu_ln_b,
                                w_out_b, w_ff1, w_ff2)
        if l + 1 < depth:
            x2, h2, w_in_b, w_out_b = _ffn(x2, l, norm_mlp, w1_b, w2_b, norm_mix, (w_in, w_out))
        else:
            (x2,) = _ffn(x2, l, norm_mlp, w1_b, w2_b, gf, ())
    return x2.reshape(batch, seq, d)
```

```python
import functools

import jax
import jax.numpy as jnp
from jax import lax
from jax.experimental import pallas as pl
from jax.experimental.pallas import tpu as pltpu

D_MODEL = 1024
SGU_HEADS = 8
HEAD_DIM = D_MODEL // SGU_HEADS
CHUNK = 128
D_FF = 4 * D_MODEL
EPS = 1e-6

SUBLANES = 8
BF16_ROWS = 16
MIX_ROWS = 1024
FFN_ROWS = 1024
GROUP_COLS = 256
HEAD_ROWS = 256
TAIL_ROWS = 256
CAST_STEPS = 8
VMEM_LIMIT_BYTES = 60 * 1024 * 1024


def _rms_norm(x, g):
    return x * lax.rsqrt(jnp.mean(x * x, axis=-1, keepdims=True) + EPS) * g


def _dot(a, b):
    return jnp.dot(a, b, preferred_element_type=jnp.float32)


def _pack_bf16(w):
    return pltpu.bitcast(w.astype(jnp.bfloat16), jnp.uint32)


def _unpack_bf16(packed):
    return pltpu.bitcast(packed, jnp.bfloat16)


def _cast_chunks(src_refs, dst_refs):
    for src, dst in zip(src_refs, dst_refs):
        dst[...] = _pack_bf16(src[...])


def _mixer_kernel(x_ref, h_src_ref, w_in_ref, conv_w_ref, sgu_w_ref, sgu_b_ref, ln_g_ref, ln_b_ref,
                  w_out_ref, w1_src, w2_src, o_ref, w1_dst, w2_dst, z_ref, *, blocks_per_seq, h_given):
    rows = x_ref.shape[0]
    half = rows // 2

    @pl.when(pl.program_id(0) % blocks_per_seq == 0)
    def _():
        z_ref[0:SUBLANES, :] = jnp.zeros((SUBLANES, D_MODEL), jnp.float32)

    _cast_chunks((w1_src, w2_src), (w1_dst, w2_dst))

    x = x_ref[...]
    head_blocks = [slice(0, HEAD_ROWS), slice(HEAD_ROWS, rows)]
    if h_given:
        h = h_src_ref[...]
    else:
        h = jnp.concatenate([_rms_norm(x[rs], h_src_ref[...]).astype(jnp.bfloat16) for rs in head_blocks], axis=0)

    def proj(k, cols, hk=h):
        return _dot(hk, _unpack_bf16(w_in_ref[:, k * D_MODEL + cols.start:k * D_MODEL + cols.stop]))

    cw = conv_w_ref[...]
    groups = [slice(j * GROUP_COLS, (j + 1) * GROUP_COLS) for j in range(D_MODEL // GROUP_COLS)]

    def pass1(cols, rs):
        n, lo = rs.stop - rs.start, SUBLANES + rs.start
        gv_rs = jax.nn.gelu(proj(4, cols, h[rs]))
        z_ref[lo:lo + n, cols] = proj(0, cols, h[rs]) * proj(2, cols, h[rs])
        p1 = proj(1, cols, h[rs])
        conv = cw[2:3, cols] * z_ref[lo:lo + n, cols]
        conv += cw[1:2, cols] * z_ref[lo - 1:lo - 1 + n, cols]
        conv += cw[0:1, cols] * z_ref[lo - 2:lo - 2 + n, cols]
        return gv_rs, jax.nn.sigmoid(proj(5, cols, h[rs])) * (p1 * conv)

    gv, gv_sum, y_a = [], 0.0, []
    for j, cols in enumerate(groups):
        blocks = head_blocks if j == 0 else [slice(0, rows)]
        parts = [pass1(cols, rs) for rs in blocks]
        gv.append(jnp.concatenate([p[0] for p in parts], axis=0))
        y_a.append(jnp.concatenate([p[1] for p in parts], axis=0))
        gv_sum += jnp.sum(gv[-1], axis=-1, keepdims=True)
    z_ref[0:SUBLANES, :] = z_ref[rows:rows + SUBLANES, :]

    mu = gv_sum * (1.0 / D_MODEL)
    gc = [g - mu for g in gv]
    var = sum(jnp.sum(c * c, axis=-1, keepdims=True) for c in gc) * (1.0 / D_MODEL)
    rstd = lax.rsqrt(var + EPS)

    t_idx = lax.broadcasted_iota(jnp.int32, (CHUNK, CHUNK), 0)
    s_idx = lax.broadcasted_iota(jnp.int32, (CHUNK, CHUNK), 1)
    causal = s_idx <= t_idx
    bias = sgu_b_ref[...]

    def spatial_gate(vn, hd):
        n = vn.shape[0] // CHUNK
        w = jnp.where(causal, sgu_w_ref[hd], 0.0).astype(jnp.bfloat16)
        v_h = jnp.concatenate([vn[c * CHUNK:(c + 1) * CHUNK] for c in range(n)], axis=1)
        m_h = _dot(w, v_h) + bias[:, hd:hd + 1]
        return jnp.concatenate([m_h[:, c * HEAD_DIM:(c + 1) * HEAD_DIM] for c in range(n)], axis=0)

    def branch_b_merge(j, rs):
        cols = groups[j]
        gu = jax.nn.gelu(proj(3, cols, h[rs]))
        vn = (gc[j][rs] * rstd[rs] * ln_g_ref[:, cols] + ln_b_ref[:, cols]).astype(jnp.bfloat16)
        mixed = jnp.concatenate(
            [spatial_gate(vn[:, k * HEAD_DIM:(k + 1) * HEAD_DIM], cols.start // HEAD_DIM + k)
             for k in range(GROUP_COLS // HEAD_DIM)], axis=1)
        gate_b = jax.nn.sigmoid(proj(6, cols, h[rs]))
        return (y_a[j][rs] + gate_b * (gu * mixed)).astype(jnp.bfloat16)

    all_rows, halves = slice(0, rows), [slice(0, half), slice(half, rows)]
    merged = [jnp.concatenate([branch_b_merge(0, rs) for rs in halves], axis=0)]
    merged += [branch_b_merge(j, all_rows) for j in range(1, len(groups) - 1)]
    last = [branch_b_merge(len(groups) - 1, rs) for rs in halves]
    merged = jnp.concatenate(merged + [jnp.concatenate(last, axis=0)], axis=1)
    for rs in [halves[0], slice(half, rows - TAIL_ROWS), slice(rows - TAIL_ROWS, rows)]:
        o_ref[rs, :] = x[rs] + _dot(merged[rs], _unpack_bf16(w_out_ref[...]))


def _ffn_kernel(x_ref, g_ref, w1_ref, w2_ref, gn_ref, *rest, final):
    n_cast = len(rest) // 2 - (0 if final else 1)
    o_ref = rest[n_cast]
    _cast_chunks(rest[:n_cast], rest[len(rest) - n_cast:])

    rows = x_ref.shape[0]
    x = x_ref[...]
    h = jnp.concatenate([_rms_norm(x[rs], g_ref[...]).astype(jnp.bfloat16)
                         for rs in (slice(0, HEAD_ROWS), slice(HEAD_ROWS, rows))], axis=0)
    hid = jnp.concatenate(
        [jnp.square(jnp.maximum(_dot(h[rs], _unpack_bf16(w1_ref[...])), 0.0)).astype(jnp.bfloat16)
         for rs in (slice(0, HEAD_ROWS), slice(HEAD_ROWS, rows))], axis=0)
    for rs in (slice(a, a + TAIL_ROWS) for a in range(0, rows, TAIL_ROWS)):
        out = x[rs] + _dot(hid[rs], _unpack_bf16(w2_ref[...]))
        normed = _rms_norm(out, gn_ref[...])
        if final:
            o_ref[rs, :] = normed
        else:
            o_ref[rs, :] = out
            rest[n_cast + 1][rs, :] = normed.astype(jnp.bfloat16)


def _resident(arr):
    zeros = (0,) * arr.ndim
    return pl.BlockSpec(arr.shape, lambda i: zeros, pipeline_mode=pl.Buffered(1))


def _layer_resident(arr, layer):
    zeros = (0,) * (arr.ndim - 1)
    return pl.BlockSpec((None,) + arr.shape[1:], lambda i: (layer,) + zeros, pipeline_mode=pl.Buffered(1))


def _cast_specs(stack, layer, steps):
    _, k, n = stack.shape
    chunk = k // steps
    assert chunk * steps == k and chunk % BF16_ROWS == 0, (stack.shape, steps)
    src = pl.BlockSpec((None, chunk, n), lambda i: (layer, i, 0))
    dst = pl.BlockSpec((chunk // 2, n), lambda i: (i, 0))
    return src, dst, jax.ShapeDtypeStruct((k // 2, n), jnp.uint32)


def _cast_kernel(*refs):
    _cast_chunks(refs[:len(refs) // 2], refs[len(refs) // 2:])


def _cast_weights(stacks, layer):
    casts = [_cast_specs(w, layer, CAST_STEPS) for w in stacks]
    return pl.pallas_call(
        _cast_kernel,
        grid=(CAST_STEPS,),
        in_specs=[c[0] for c in casts],
        out_specs=[c[1] for c in casts],
        out_shape=[c[2] for c in casts],
        compiler_params=pltpu.CompilerParams(dimension_semantics=("arbitrary",)),
        name="cast",
    )(*stacks)


def _mixer(x2, h2, layer, seq, g, w_in_b, conv_w, sgu_w, sgu_b_t, ln_g, ln_b, w_out_b, w_ff1, w_ff2):
    tokens, d = x2.shape
    rows = MIX_ROWS
    steps = tokens // rows
    xspec = pl.BlockSpec((rows, d), lambda i: (i, 0))
    stacked = lambda p: _layer_resident(p, layer)
    casts = [_cast_specs(w, layer, steps) for w in (w_ff1, w_ff2)]
    h_src, h_src_spec = (g, stacked(g)) if h2 is None else (h2, xspec)
    return pl.pallas_call(
        functools.partial(_mixer_kernel, blocks_per_seq=seq // rows, h_given=h2 is not None),
        grid=(steps,),
        in_specs=[xspec, h_src_spec, _resident(w_in_b), stacked(conv_w), stacked(sgu_w), stacked(sgu_b_t),
                  stacked(ln_g), stacked(ln_b), _resident(w_out_b)] + [c[0] for c in casts],
        out_specs=[xspec] + [c[1] for c in casts],
        out_shape=[jax.ShapeDtypeStruct(x2.shape, x2.dtype)] + [c[2] for c in casts],
        scratch_shapes=[pltpu.VMEM((rows + SUBLANES, d), jnp.float32)],
        compiler_params=pltpu.CompilerParams(
            dimension_semantics=("arbitrary",), vmem_limit_bytes=VMEM_LIMIT_BYTES),
        name="mixer",
    )(x2, h_src, w_in_b, conv_w, sgu_w, sgu_b_t, ln_g, ln_b, w_out_b, w_ff1, w_ff2)


def _ffn(x2, layer, g, w1_b, w2_b, g_next, next_weights):
    tokens, d = x2.shape
    rows = FFN_ROWS
    steps = tokens // rows
    xspec = pl.BlockSpec((rows, d), lambda i: (i, 0))
    casts = [_cast_specs(w, layer + 1, steps) for w in next_weights]
    final = not next_weights
    gn_spec = _resident(g_next) if final else _layer_resident(g_next, layer + 1)
    h_out = [] if final else [(xspec, jax.ShapeDtypeStruct(x2.shape, jnp.bfloat16))]
    return pl.pallas_call(
        functools.partial(_ffn_kernel, final=final),
        grid=(steps,),
        in_specs=[xspec, _layer_resident(g, layer), _resident(w1_b), _resident(w2_b), gn_spec]
                 + [c[0] for c in casts],
        out_specs=[xspec] + [o[0] for o in h_out] + [c[1] for c in casts],
        out_shape=[jax.ShapeDtypeStruct(x2.shape, x2.dtype)] + [o[1] for o in h_out] + [c[2] for c in casts],
        compiler_params=pltpu.CompilerParams(
            dimension_semantics=("arbitrary",), vmem_limit_bytes=VMEM_LIMIT_BYTES),
        name="ffn_final" if final else "ffn",
    )(x2, g, w1_b, w2_b, g_next, *next_weights)


def kernel(x, norm_mix, w_in, conv_w, sgu_w, sgu_b, sgu_ln_g, sgu_ln_b, w_out, norm_mlp, w_ff1, w_ff2, final_norm):
    batch, seq, d = x.shape
    depth = w_in.shape[0]
    row = lambda p: p.reshape(depth, 1, d)
    norm_mix, norm_mlp, sgu_ln_g, sgu_ln_b = (row(p) for p in (norm_mix, norm_mlp, sgu_ln_g, sgu_ln_b))
    sgu_b_t = jnp.swapaxes(sgu_b, 1, 2)
    gf = final_norm.reshape(1, d)
    x2 = x.reshape(batch * seq, d)
    w_in_b, w_out_b = _cast_weights((w_in, w_out), 0)
    h2 = None
    for l in range(depth):
        x2, w1_b, w2_b = _mixer(x2, h2, l, seq, norm_mix, w_in_b, conv_w, sgu_w, sgu_b_t, sgu_ln_g, sgu_ln_b,
                                w_out_b, w_ff1, w_ff2)
        if l + 1 < depth:
            x2, h2, w_in_b, w_out_b = _ffn(x2, l, norm_mlp, w1_b, w2_b, norm_mix, (w_in, w_out))
        else:
            (x2,) = _ffn(x2, l, norm_mlp, w1_b, w2_b, gf, ())
    return x2.reshape(batch, seq, d)
```
